```python
import math
import jax
import jax.numpy as jnp
from jax import lax
import numpy as np

D_MODEL = 2048
BATCH = 4
SEQ = 4096
DEPTH = 1

CHUNK = 64
NORM_EPS = 1e-5
D_MIX = 2 * D_MODEL

SSD_WIDTH = D_MIX // 2
SSD_HEAD_DIM = 64
SSD_HEADS = SSD_WIDTH // SSD_HEAD_DIM
SSD_GROUPS = 4
SSD_HEADS_PER_GROUP = SSD_HEADS // SSD_GROUPS
SSD_STATE = 128
CONV_WIDTH = 4
SSD_CONV_DIM = SSD_WIDTH + 2 * SSD_GROUPS * SSD_STATE
DT_MIN = 0.001
DT_MAX = 0.1

GLA_VALUE_WIDTH = D_MIX - SSD_WIDTH
GLA_HEADS = 4
GLA_KEY_WIDTH = GLA_VALUE_WIDTH // 2
GLA_HEAD_K = GLA_KEY_WIDTH // GLA_HEADS
GLA_HEAD_V = GLA_VALUE_WIDTH // GLA_HEADS
GLA_GATE_RANK = 16
GLA_GATE_NORMALIZER = 16.0

IN_PROJ_WIDTHS = (SSD_WIDTH, SSD_CONV_DIM, SSD_HEADS, GLA_KEY_WIDTH, GLA_KEY_WIDTH, GLA_VALUE_WIDTH, GLA_VALUE_WIDTH, GLA_GATE_RANK)
IN_PROJ_DIM = SSD_WIDTH + SSD_CONV_DIM + SSD_HEADS + 2 * GLA_KEY_WIDTH + 2 * GLA_VALUE_WIDTH + GLA_GATE_RANK

N_EXPERTS = 32
TOP_K = 4
D_FF = D_MODEL
SWIGLU_LIMIT = 7.0
SWIGLU_ALPHA = 1.702
EXPERT_BLOCK = 256

kernel_name = 'hybrid_ssd_gla_moe_block'


def rms_norm(x, w):
    xf = x.astype(jnp.float32)
    y = xf * lax.rsqrt(jnp.mean(xf * xf, axis=-1, keepdims=True) + NORM_EPS)
    return (y * w.astype(jnp.float32)).astype(x.dtype)


def split_columns(u):
    parts, start = [], 0
    for width in IN_PROJ_WIDTHS:
        parts.append(u[..., start:start + width])
        start += width
    return parts


def ssd_mixer(z, xbc, dt_raw, conv_w, conv_b, dt_bias, a_log, d_skip, norm_w):
    b, t, _ = xbc.shape
    nc = t // CHUNK
    G, R, P, N = SSD_GROUPS, SSD_HEADS_PER_GROUP, SSD_HEAD_DIM, SSD_STATE
    xbc = lax.conv_general_dilated(xbc, conv_w[:, None, :], window_strides=(1,), padding=[(CONV_WIDTH - 1, 0)], dimension_numbers=('NWC', 'WIO', 'NWC'), feature_group_count=SSD_CONV_DIM) + conv_b
    xbc = jax.nn.silu(xbc).astype(jnp.float32)
    xs = xbc[..., :SSD_WIDTH].reshape(b, nc, CHUNK, G, R, P)
    bm = xbc[..., SSD_WIDTH:SSD_WIDTH + G * N].reshape(b, nc, CHUNK, G, N)
    cm = xbc[..., SSD_WIDTH + G * N:].reshape(b, nc, CHUNK, G, N)
    dt = jax.nn.softplus(dt_raw.astype(jnp.float32) + dt_bias.astype(jnp.float32)).reshape(b, nc, CHUNK, G, R)
    a = -jnp.exp(a_log.astype(jnp.float32)).reshape(G, R)
    la_cs = jnp.cumsum(dt * a, axis=2)
    xd = xs * dt[..., None]
    causal = jnp.tril(jnp.ones((CHUNK, CHUNK), dtype=bool))
    seg = la_cs[:, :, :, None] - la_cs[:, :, None, :]
    decay_ls = jnp.exp(jnp.where(causal[None, None, :, :, None, None], seg, -jnp.inf))
    cb = jnp.einsum('bclgn,bcsgn->bclsg', cm, bm)
    y_diag = jnp.einsum('bclsgr,bcsgrp->bclgrp', cb[..., None] * decay_ls, xd)
    decay_to_end = jnp.exp(la_cs[:, :, -1:] - la_cs)
    chunk_states = jnp.einsum('bclgn,bclgr,bclgrp->bcgrpn', bm, decay_to_end, xd)
    chunk_decay = jnp.exp(la_cs[:, :, -1])

    def step(state, inp):
        dec, st = inp
        return dec[..., None, None] * state + st, state

    init = jnp.zeros((b, G, R, P, N), jnp.float32)
    _, states_in = lax.scan(step, init, (jnp.moveaxis(chunk_decay, 1, 0), jnp.moveaxis(chunk_states, 1, 0)))
    states_in = jnp.moveaxis(states_in, 0, 1)
    y_off = jnp.einsum('bclgn,bcgrpn,bclgr->bclgrp', cm, states_in, jnp.exp(la_cs))
    y = y_diag + y_off + d_skip.astype(jnp.float32).reshape(G, R)[:, :, None] * xs
    y = y.reshape(b, t, SSD_WIDTH) * jax.nn.silu(z.astype(jnp.float32))
    yg = y.reshape(b, t, G, SSD_WIDTH // G)
    yg = yg * lax.rsqrt(jnp.mean(yg * yg, axis=-1, keepdims=True) + NORM_EPS)
    return (yg.reshape(b, t, SSD_WIDTH) * norm_w.astype(jnp.float32)).astype(z.dtype)


def gla_mixer(q, k, v, g, gate_low, gate_up_w, gate_up_b, norm_w):
    b, t, _ = q.shape
    nc = t // CHUNK
    H, K, V = GLA_HEADS, GLA_HEAD_K, GLA_HEAD_V
    log_forget = jax.nn.log_sigmoid((gate_low @ gate_up_w + gate_up_b).astype(jnp.float32)) / GLA_GATE_NORMALIZER

    def to_chunks(u, d):
        return jnp.moveaxis(u.astype(jnp.float32).reshape(b, nc, CHUNK, H, d), 1, 0)

    qc = to_chunks(q, K) * (K ** -0.5)
    kc = to_chunks(k, K)
    vc = to_chunks(v, V)
    gc = to_chunks(log_forget, K)
    causal = jnp.tril(jnp.ones((CHUNK, CHUNK), dtype=bool))

    def step(state, inp):
        qb, kb, vb, gb = inp
        cum = jnp.cumsum(gb, axis=1)
        o_inter = jnp.einsum('blhk,bhkv->blhv', qb * jnp.exp(cum), state)
        rel = jnp.exp(jnp.where(causal[None, :, :, None, None], cum[:, :, None] - cum[:, None, :], -jnp.inf))
        scores = jnp.einsum('blhk,bshk,blshk->bhls', qb, kb, rel)
        o_intra = jnp.einsum('bhls,bshv->blhv', scores, vb)
        last = cum[:, -1]
        new_state = jnp.exp(last)[..., None] * state + jnp.einsum('bshk,bshv->bhkv', kb * jnp.exp(last[:, None] - cum), vb)
        return new_state, o_inter + o_intra

    init = jnp.zeros((b, H, K, V), jnp.float32)
    _, o = lax.scan(step, init, (qc, kc, vc, gc))
    o = jnp.moveaxis(o, 0, 1).reshape(b, t, H, V)
    o = o * lax.rsqrt(jnp.mean(o * o, axis=-1, keepdims=True) + NORM_EPS) * norm_w.astype(jnp.float32)
    o = o * jax.nn.silu(g.astype(jnp.float32).reshape(b, t, H, V))
    return o.reshape(b, t, GLA_VALUE_WIDTH).astype(q.dtype)


def moe_ffn(h, router_w, router_b, w_gate_up, b_gate_up, w_down, b_down):
    b, t, d = h.shape
    tokens = h.reshape(b * t, d)
    n_tok = b * t
    n_assign = n_tok * TOP_K
    logits = (tokens @ router_w + router_b).astype(jnp.float32)
    top_logits, top_idx = lax.top_k(logits, TOP_K)
    top_w = jax.nn.softmax(top_logits, axis=-1)
    flat_e = top_idx.reshape(-1)
    order = jnp.argsort(flat_e)
    sorted_e = flat_e[order]
    sorted_tok = order // TOP_K
    sorted_w = top_w.reshape(-1)[order]
    counts = jnp.bincount(flat_e, length=N_EXPERTS)
    padded = (counts + EXPERT_BLOCK - 1) // EXPERT_BLOCK * EXPERT_BLOCK
    pad_end = jnp.cumsum(padded)
    pad_start = pad_end - padded
    sort_start = jnp.cumsum(counts) - counts
    dest = pad_start[sorted_e] + jnp.arange(n_assign) - sort_start[sorted_e]
    n_blocks = -(-(n_assign + N_EXPERTS * (EXPERT_BLOCK - 1)) // EXPERT_BLOCK)
    n_rows = n_blocks * EXPERT_BLOCK
    row_tok = jnp.full((n_rows,), n_tok, dtype=jnp.int32).at[dest].set(sorted_tok.astype(jnp.int32))
    row_w = jnp.zeros((n_rows,), jnp.float32).at[dest].set(sorted_w)
    block_e = jnp.minimum(jnp.searchsorted(pad_end, jnp.arange(n_blocks) * EXPERT_BLOCK, side='right'), N_EXPERTS - 1)
    tokens_ext = jnp.concatenate([tokens, jnp.zeros((1, d), tokens.dtype)], axis=0)

    def expert_block(args):
        tok_idx, e = args
        xb = tokens_ext[tok_idx]
        gu = xb @ w_gate_up[e] + b_gate_up[e]
        gate = jnp.minimum(gu[:, :D_FF], SWIGLU_LIMIT)
        up = jnp.clip(gu[:, D_FF:], -SWIGLU_LIMIT, SWIGLU_LIMIT)
        act = (up + 1.0) * gate * jax.nn.sigmoid(SWIGLU_ALPHA * gate)
        return act @ w_down[e] + b_down[e]

    out = lax.map(expert_block, (row_tok.reshape(n_blocks, EXPERT_BLOCK), block_e))
    out = out.reshape(n_rows, d) * row_w[:, None].astype(out.dtype)
    y = jnp.zeros((n_tok + 1, d), out.dtype).at[row_tok].add(out)[:n_tok]
    return y.reshape(b, t, d).astype(h.dtype)


def setup_inputs(seed: int = 0) -> dict:
    key = jax.random.key(seed)
    ks = jax.random.split(key, 22)
    nrm = jax.random.normal
    u_dt = jax.random.uniform(ks[5], (DEPTH, SSD_HEADS))
    dt = jnp.exp(u_dt * (math.log(DT_MAX) - math.log(DT_MIN)) + math.log(DT_MIN))
    return {
        'x': nrm(ks[0], (BATCH, SEQ, D_MODEL), jnp.float32),
        'norm_mix_w': 1.0 + 0.02 * nrm(ks[1], (DEPTH, D_MODEL)),
        'in_proj_w': nrm(ks[2], (DEPTH, D_MODEL, IN_PROJ_DIM)) * D_MODEL ** -0.5,
        'conv_w': nrm(ks[3], (DEPTH, CONV_WIDTH, SSD_CONV_DIM)) * CONV_WIDTH ** -0.5,
        'conv_b': 0.02 * nrm(ks[4], (DEPTH, SSD_CONV_DIM)),
        'dt_bias': dt + jnp.log(-jnp.expm1(-dt)),
        'a_log': jnp.log(jax.random.uniform(ks[6], (DEPTH, SSD_HEADS), minval=1.0, maxval=16.0)),
        'd_skip': 1.0 + 0.02 * nrm(ks[7], (DEPTH, SSD_HEADS)),
        'ssd_norm_w': 1.0 + 0.02 * nrm(ks[8], (DEPTH, SSD_WIDTH)),
        'gate_up_w': nrm(ks[9], (DEPTH, GLA_GATE_RANK, GLA_KEY_WIDTH)) * GLA_GATE_RANK ** -0.5,
        'gate_up_b': 0.02 * nrm(ks[10], (DEPTH, GLA_KEY_WIDTH)),
        'gla_norm_w': 1.0 + 0.02 * nrm(ks[11], (DEPTH, GLA_HEAD_V)),
        'out_proj_w': nrm(ks[12], (DEPTH, D_MIX, D_MODEL)) * D_MIX ** -0.5,
        'norm_ffn_w': 1.0 + 0.02 * nrm(ks[13], (DEPTH, D_MODEL)),
        'router_w': nrm(ks[14], (DEPTH, D_MODEL, N_EXPERTS)) * D_MODEL ** -0.5,
        'router_b': 0.01 * nrm(ks[15], (DEPTH, N_EXPERTS)),
        'w_gate_up': nrm(ks[16], (DEPTH, N_EXPERTS, D_MODEL, 2 * D_FF)) * D_MODEL ** -0.5,
        'b_gate_up': 0.01 * nrm(ks[17], (DEPTH, N_EXPERTS, 2 * D_FF)),
        'w_down': nrm(ks[18], (DEPTH, N_EXPERTS, D_FF, D_MODEL)) * D_FF ** -0.5,
        'b_down': 0.01 * nrm(ks[19], (DEPTH, N_EXPERTS, D_MODEL)),
        'final_norm_w': 1.0 + 0.02 * nrm(ks[20], (D_MODEL,)),
    }


def reference(x, norm_mix_w, in_proj_w, conv_w, conv_b, dt_bias, a_log, d_skip, ssd_norm_w, gate_up_w, gate_up_b, gla_norm_w, out_proj_w, norm_ffn_w, router_w, router_b, w_gate_up, b_gate_up, w_down, b_down, final_norm_w):
    h = x
    for i in range(DEPTH):
        n1 = rms_norm(h, norm_mix_w[i])
        z, xbc, dt_raw, q, k, v, g, gate_low = split_columns(n1 @ in_proj_w[i])
        y_ssd = ssd_mixer(z, xbc, dt_raw, conv_w[i], conv_b[i], dt_bias[i], a_log[i], d_skip[i], ssd_norm_w[i])
        y_gla = gla_mixer(q, k, v, g, gate_low, gate_up_w[i], gate_up_b[i], gla_norm_w[i])
        h = h + jnp.concatenate([y_ssd, y_gla], axis=-1) @ out_proj_w[i]
        h = h + moe_ffn(rms_norm(h, norm_ffn_w[i]), router_w[i], router_b[i], w_gate_up[i], b_gate_up[i], w_down[i], b_down[i])
    return rms_norm(h, final_norm_w)
```

```python
import functools

import jax
import jax.numpy as jnp
from jax import lax
from jax.experimental import pallas as pl
from jax.experimental.pallas import tpu as pltpu

F32 = jnp.float32
BF16 = jnp.bfloat16

D_MODEL = 2048
NORM_EPS = 1e-5

SSD_WIDTH = 2048
SSD_HEAD_DIM = 64
SSD_HEADS = 32
SSD_GROUPS = 4
SSD_STATE = 128
CONV_WIDTH = 4
SSD_GROUP_WIDTH = SSD_WIDTH // SSD_GROUPS
SSD_BC_WIDTH = 2 * SSD_GROUPS * SSD_STATE
SSD_CONV_DIM = SSD_WIDTH + SSD_BC_WIDTH

GLA_HEADS = 4
GLA_KEY_WIDTH = 1024
GLA_VALUE_WIDTH = 2048
GLA_HEAD_K = 256
GLA_HEAD_V = 512
GLA_GATE_RANK = 16
GLA_GATE_NORMALIZER = 16.0

N_EXPERTS = 32
TOP_K = 4
D_FF = 2048
SWIGLU_LIMIT = 7.0
SWIGLU_ALPHA = 1.702

LANES = 128
SUBLANES = 8
VMEM_LIMIT_BYTES = 56 * 1024 * 1024

U_WIDTH = 11264
SMALL_WIDTH = LANES
SMALL_DT_OFF = 0
SMALL_GATE_OFF = SSD_HEADS

MIX_CHUNK = 128
GLA_SUB = 16
GLA_EXP_CLAMP = 80.0

ROW_BLOCK = 256
SUPER_ROWS = 1024
SUB_PER_SUPER = SUPER_ROWS // ROW_BLOCK
FFN_TILE = 512
TOKEN_TILE = 256


def _split2(v):
    hi = v.astype(BF16)
    lo = (v - hi.astype(F32)).astype(BF16)
    return hi, lo


def _split3(v):
    hi = v.astype(BF16)
    r = v - hi.astype(F32)
    mid = r.astype(BF16)
    lo = (r - mid.astype(F32)).astype(BF16)
    return hi, mid, lo


def _dot(a, b):
    return jnp.dot(a, b, preferred_element_type=F32)


def _dot_nt(a, b):
    return lax.dot_general(a, b, (((1,), (1,)), ((), ())), preferred_element_type=F32)


def _dot_exact_lhs(mat01, v):
    hi, mid, lo = _split3(v)
    return _dot(mat01, hi) + _dot(mat01, mid) + _dot(mat01, lo)


def _dot_hilo(a, b_hi, b_lo):
    a_hi, a_lo = _split2(a)
    return _dot(a_hi, b_hi) + _dot(a_lo, b_hi) + _dot(a_hi, b_lo)


def _silu(v):
    return v * jax.nn.sigmoid(v)


def _softplus(v):
    return jnp.maximum(v, 0.0) + jnp.log1p(jnp.exp(-jnp.abs(v)))


def _log_sigmoid(v):
    return jnp.minimum(v, 0.0) - jnp.log1p(jnp.exp(-jnp.abs(v)))


def _in_proj_kernel(x_ref, nw_ref, w_ref, wsh_ref, wsl_ref, u_ref, small_ref, n1_ref):
    @pl.when(pl.program_id(1) == 0)
    def _():
        x = x_ref[...]
        ms = jnp.mean(x * x, axis=-1, keepdims=True)
        n1 = x * lax.rsqrt(ms + NORM_EPS) * nw_ref[...]
        n1_ref[...] = n1.astype(BF16)
        small_ref[...] = _dot_hilo(n1, wsh_ref[...], wsl_ref[...])

    u_ref[...] = _dot(n1_ref[...], w_ref[...]).astype(BF16)


def _in_proj(x2, norm_w, w_main, ws_hi, ws_lo, *, tm, tn):
    m = x2.shape[0]
    grid = (m // tm, U_WIDTH // tn)
    return pl.pallas_call(
        _in_proj_kernel,
        grid=grid,
        in_specs=[
            pl.BlockSpec((tm, D_MODEL), lambda i, j: (i, 0)),
            pl.BlockSpec((1, D_MODEL), lambda i, j: (0, 0)),
            pl.BlockSpec((D_MODEL, tn), lambda i, j: (0, j)),
            pl.BlockSpec((D_MODEL, SMALL_WIDTH), lambda i, j: (0, 0)),
            pl.BlockSpec((D_MODEL, SMALL_WIDTH), lambda i, j: (0, 0)),
        ],
        out_specs=[
            pl.BlockSpec((tm, tn), lambda i, j: (i, j)),
            pl.BlockSpec((tm, SMALL_WIDTH), lambda i, j: (i, 0)),
        ],
        out_shape=[
            jax.ShapeDtypeStruct((m, U_WIDTH), BF16),
            jax.ShapeDtypeStruct((m, SMALL_WIDTH), F32),
        ],
        scratch_shapes=[pltpu.VMEM((tm, D_MODEL), BF16)],
        compiler_params=pltpu.CompilerParams(
            dimension_semantics=("arbitrary", "arbitrary"),
            vmem_limit_bytes=VMEM_LIMIT_BYTES),
        name="in_proj",
    )(x2, norm_w, w_main, ws_hi, ws_lo)


def _ssd_kernel(z_ref, x_ref, bc_ref, sm_ref, convw_ref, convb_ref, dtb_ref, alog_ref,
                dsk_ref, nw_ref, e2_ref, tri_ref, o_ref,
                ext_ref, xc_ref, xd_ref, xdd_ref, y_ref, state_ref):
    L = MIX_CHUNK
    G, N, P = SSD_GROUPS, SSD_STATE, SSD_HEAD_DIM
    GW = SSD_GROUP_WIDTH

    @pl.when(pl.program_id(1) == 0)
    def _():
        ext_ref[0:SUBLANES, :] = jnp.zeros((SUBLANES, SSD_CONV_DIM), F32)
        state_ref[...] = jnp.zeros(state_ref.shape, F32)

    ext_ref[SUBLANES:SUBLANES + L, 0:SSD_WIDTH] = x_ref[...].astype(F32)
    ext_ref[SUBLANES:SUBLANES + L, SSD_WIDTH:SSD_CONV_DIM] = bc_ref[...].astype(F32)
    slab = 512
    for s in range(SSD_CONV_DIM // slab):
        cols = slice(s * slab, (s + 1) * slab)
        acc = jnp.broadcast_to(convb_ref[:, cols], (L, slab))
        for w in range(CONV_WIDTH):
            off = SUBLANES - (CONV_WIDTH - 1) + w
            acc = acc + convw_ref[w:w + 1, cols] * ext_ref[off:off + L, cols]
        xc_ref[:, cols] = _silu(acc)
    ext_ref[0:SUBLANES, :] = ext_ref[L:L + SUBLANES, :]

    lane = lax.broadcasted_iota(jnp.int32, (1, LANES), 1)
    a = jnp.where(lane < SSD_HEADS, -jnp.exp(alog_ref[...]), 0.0)
    dt = _softplus(sm_ref[...] + dtb_ref[...])
    la = _dot_exact_lhs(tri_ref[...], dt * a)
    la_t = la.T
    la_end = la[L - 1:L, :]

    def expand(v):
        hi, lo = _split2(v)
        return _dot(jnp.concatenate([hi, lo], axis=1), e2_ref[...])

    dt_e = expand(dt)
    ela_e = expand(jnp.exp(la))
    dte_e = expand(jnp.exp(la_end - la))
    cd_e = expand(jnp.broadcast_to(jnp.exp(la_end), (SUBLANES, LANES)))[0:1, :]

    for s in range(SSD_WIDTH // slab):
        cols = slice(s * slab, (s + 1) * slab)
        xd = xc_ref[:, cols] * dt_e[:, cols]
        xd_ref[:, cols] = xd
        xdd_ref[:, cols] = (xd * dte_e[:, cols]).astype(BF16)

    row = lax.broadcasted_iota(jnp.int32, (L, L), 0)
    col = lax.broadcasted_iota(jnp.int32, (L, L), 1)
    causal = row >= col
    lane_l = lax.broadcasted_iota(jnp.int32, (L, LANES), 1)

    for g in range(G):
        bm = xc_ref[:, SSD_WIDTH + g * N:SSD_WIDTH + (g + 1) * N]
        cm = xc_ref[:, SSD_WIDTH + G * N + g * N:SSD_WIDTH + G * N + (g + 1) * N]
        cm_b = cm.astype(BF16)
        cb = _dot_nt(cm_b, bm.astype(BF16))
        gcols = slice(g * GW, (g + 1) * GW)
        y_off = _dot(cm_b, state_ref[g].astype(BF16)) * ela_e[:, gcols]
        for p in range(GW // LANES):
            h0 = g * (GW // P) + 2 * p
            ms = []
            for h in (h0, h0 + 1):
                seg = la[:, h:h + 1] - la_t[h:h + 1, :]
                dec = jnp.where(causal, jnp.exp(jnp.minimum(seg, 0.0)), 0.0)
                ms.append((cb * dec).astype(BF16))
            pcols = slice(g * GW + p * LANES, g * GW + (p + 1) * LANES)
            xp = xd_ref[:, pcols]
            rhs = jnp.concatenate([jnp.where(lane_l < P, xp, 0.0),
                                   jnp.where(lane_l >= P, xp, 0.0)], axis=0).astype(BF16)
            y_pair = _dot(jnp.concatenate(ms, axis=1), rhs)
            y_ref[:, pcols] = (y_pair + y_off[:, p * LANES:(p + 1) * LANES]
                               + dsk_ref[:, pcols] * xc_ref[:, pcols])
        upd = _dot(bm.T.astype(BF16), xdd_ref[:, gcols])
        state_ref[g] = cd_e[:, gcols] * state_ref[g] + upd

    for g in range(G):
        gcols = slice(g * GW, (g + 1) * GW)
        y = y_ref[:, gcols] * _silu(z_ref[:, gcols].astype(F32))
        ms = jnp.mean(y * y, axis=-1, keepdims=True)
        o_ref[:, gcols] = (y * lax.rsqrt(ms + NORM_EPS) * nw_ref[:, gcols]).astype(BF16)


def _ssd(u, small, conv_w, conv_b, dtb, alog, dsk_e, norm_w, e2, tri, *, batch, seq):
    L = MIX_CHUNK
    nc = seq // L
    m = batch * seq

    def rows(b, c):
        return b * nc + c

    const = lambda shape: pl.BlockSpec(shape, lambda b, c: (0,) * len(shape))
    return pl.pallas_call(
        _ssd_kernel,
        grid=(batch, nc),
        in_specs=[
            pl.BlockSpec((L, SSD_WIDTH), lambda b, c: (rows(b, c), 0)),
            pl.BlockSpec((L, SSD_WIDTH), lambda b, c: (rows(b, c), 1)),
            pl.BlockSpec((L, SSD_BC_WIDTH), lambda b, c: (rows(b, c), 8)),
            pl.BlockSpec((L, SMALL_WIDTH), lambda b, c: (rows(b, c), 0)),
            const((CONV_WIDTH, SSD_CONV_DIM)),
            const((1, SSD_CONV_DIM)),
            const((1, LANES)),
            const((1, LANES)),
            const((1, SSD_WIDTH)),
            const((1, SSD_WIDTH)),
            const((2 * LANES, SSD_WIDTH)),
            const((L, L)),
        ],
        out_specs=pl.BlockSpec((L, SSD_WIDTH), lambda b, c: (rows(b, c), 0)),
        out_shape=jax.ShapeDtypeStruct((m, SSD_WIDTH), BF16),
        scratch_shapes=[
            pltpu.VMEM((L + 2 * SUBLANES, SSD_CONV_DIM), F32),
            pltpu.VMEM((L, SSD_CONV_DIM), F32),
            pltpu.VMEM((L, SSD_WIDTH), F32),
            pltpu.VMEM((L, SSD_WIDTH), BF16),
            pltpu.VMEM((L, SSD_WIDTH), F32),
            pltpu.VMEM((SSD_GROUPS, SSD_STATE, SSD_GROUP_WIDTH), F32),
        ],
        compiler_params=pltpu.CompilerParams(
            dimension_semantics=("arbitrary", "arbitrary"),
            vmem_limit_bytes=VMEM_LIMIT_BYTES),
        name="ssd",
    )(u, u, u, small, conv_w, conv_b, dtb, alog, dsk_e, norm_w, e2, tri)


def _gla_kernel(q_ref, k_ref, v_ref, g_ref, sm_ref, wuh_ref, wul_ref, bup_ref, nw_ref,
                tri_ref, o_ref, cum_ref, s_ref, st_ref):
    L, C = MIX_CHUNK, GLA_SUB
    K, V = GLA_HEAD_K, GLA_HEAD_V

    @pl.when(pl.program_id(1) == 0)
    def _():
        st_ref[...] = jnp.zeros(st_ref.shape, F32)

    pre = _dot_hilo(sm_ref[...], wuh_ref[...], wul_ref[...]) + bup_ref[...]
    gk = _log_sigmoid(pre) * (1.0 / GLA_GATE_NORMALIZER)
    cum_ref[...] = _dot_exact_lhs(tri_ref[...], gk)

    row_c = lax.broadcasted_iota(jnp.int32, (C, L), 0)
    col_c = lax.broadcasted_iota(jnp.int32, (C, L), 1)
    for h in range(GLA_HEADS):
        ks = slice(h * K, (h + 1) * K)
        vs = slice(h * V, (h + 1) * V)
        cum = cum_ref[:, ks]
        cum_last = cum_ref[L - 1:L, ks]
        qh = q_ref[:, ks].astype(F32) * (K ** -0.5)
        kh = k_ref[:, ks].astype(F32)
        vh = v_ref[:, vs]
        for blk in range(L // C):
            rows = slice(blk * C, (blk + 1) * C)
            if blk == 0:
                ref_row = jnp.zeros((1, K), F32)
            else:
                ref_row = cum_ref[blk * C - 1:blk * C, ks]
            q_b = (qh[rows] * jnp.exp(cum_ref[rows, ks] - ref_row)).astype(BF16)
            k_b = (kh * jnp.exp(jnp.minimum(ref_row - cum, GLA_EXP_CLAMP))).astype(BF16)
            s_b = _dot_nt(q_b, k_b)
            s_ref[rows, :] = jnp.where(col_c <= row_c + blk * C, s_b, 0.0)
        o_intra = _dot(s_ref[...].astype(BF16), vh)
        st = st_ref[h]
        o_inter = _dot_nt((qh * jnp.exp(cum)).astype(BF16), st.astype(BF16))
        k_dec = (kh * jnp.exp(cum_last - cum)).astype(BF16)
        st_ref[h] = st * jnp.exp(cum_last) + _dot(vh.astype(F32).T.astype(BF16), k_dec)
        o = o_inter + o_intra
        ms = jnp.mean(o * o, axis=-1, keepdims=True)
        o = o * lax.rsqrt(ms + NORM_EPS) * nw_ref[...] * _silu(g_ref[:, vs].astype(F32))
        o_ref[:, vs] = o.astype(BF16)


def _gla(u, small, wup_hi, wup_lo, bup, norm_w, tri, *, batch, seq):
    L = MIX_CHUNK
    nc = seq // L
    m = batch * seq

    def rows(b, c):
        return b * nc + c

    const = lambda shape: pl.BlockSpec(shape, lambda b, c: (0,) * len(shape))
    return pl.pallas_call(
        _gla_kernel,
        grid=(batch, nc),
        in_specs=[
            pl.BlockSpec((L, GLA_KEY_WIDTH), lambda b, c: (rows(b, c), 9)),
            pl.BlockSpec((L, GLA_KEY_WIDTH), lambda b, c: (rows(b, c), 10)),
            pl.BlockSpec((L, GLA_VALUE_WIDTH), lambda b, c: (rows(b, c), 2)),
            pl.BlockSpec((L, GLA_VALUE_WIDTH), lambda b, c: (rows(b, c), 3)),
            pl.BlockSpec((L, SMALL_WIDTH), lambda b, c: (rows(b, c), 0)),
            const((SMALL_WIDTH, GLA_KEY_WIDTH)),
            const((SMALL_WIDTH, GLA_KEY_WIDTH)),
            const((1, GLA_KEY_WIDTH)),
            const((1, GLA_HEAD_V)),
            const((L, L)),
        ],
        out_specs=pl.BlockSpec((L, GLA_VALUE_WIDTH), lambda b, c: (rows(b, c), 0)),
        out_shape=jax.ShapeDtypeStruct((m, GLA_VALUE_WIDTH), BF16),
        scratch_shapes=[
            pltpu.VMEM((L, GLA_KEY_WIDTH), F32),
            pltpu.VMEM((L, L), F32),
            pltpu.VMEM((GLA_HEADS, GLA_HEAD_V, GLA_HEAD_K), F32),
        ],
        compiler_params=pltpu.CompilerParams(
            dimension_semantics=("arbitrary", "arbitrary"),
            vmem_limit_bytes=VMEM_LIMIT_BYTES),
        name="gla",
    )(u, u, u, u, small, wup_hi, wup_lo, bup, norm_w, tri)


def _out_proj_kernel(ys_ref, yg_ref, x_ref, wo1_ref, wo2_ref, nw_ref, wrh_ref, wrl_ref,
                     rb_ref, stri_ref, h1_ref, n2_ref, idx_ref, wt_ref, rank_ref, cnt_ref,
                     run_ref):
    tm = x_ref.shape[0]

    @pl.when(pl.program_id(0) == 0)
    def _():
        run_ref[...] = jnp.zeros(run_ref.shape, F32)

    h = x_ref[...] + _dot(ys_ref[...], wo1_ref[...]) + _dot(yg_ref[...], wo2_ref[...])
    h1_ref[...] = h
    ms = jnp.mean(h * h, axis=-1, keepdims=True)
    n2 = h * lax.rsqrt(ms + NORM_EPS) * nw_ref[...]
    n2_ref[...] = n2

    lane = lax.broadcasted_iota(jnp.int32, (tm, LANES), 1)
    lane_f = lane.astype(F32)
    logits = _dot_hilo(n2, wrh_ref[...], wrl_ref[...]) + rb_ref[...]
    work = jnp.where(lane < N_EXPERTS, logits, -jnp.inf)
    vals, idxs, sels = [], [], []
    for _ in range(TOP_K):
        top = jnp.max(work, axis=-1, keepdims=True)
        idx = jnp.min(jnp.where(work == top, lane_f, float(LANES)), axis=-1, keepdims=True)
        sel = lane_f == idx
        vals.append(top)
        idxs.append(idx)
        sels.append(sel)
        work = jnp.where(sel, -jnp.inf, work)
    exps = [jnp.exp(v - vals[0]) for v in vals]
    denom = exps[0] + exps[1] + exps[2] + exps[3]

    chosen = jnp.zeros((tm, LANES), F32)
    for sel in sels:
        chosen = chosen + jnp.where(sel, 1.0, 0.0)
    before = run_ref[0:1, :] + _dot(stri_ref[...], chosen.astype(BF16))

    idx_out = jnp.zeros((tm, LANES), F32)
    wt_out = jnp.zeros((tm, LANES), F32)
    rank_out = jnp.zeros((tm, LANES), F32)
    for j in range(TOP_K):
        rank_j = jnp.sum(jnp.where(sels[j], before, 0.0), axis=-1, keepdims=True)
        idx_out = jnp.where(lane == j, idxs[j], idx_out)
        wt_out = jnp.where(lane == j, exps[j] / denom, wt_out)
        rank_out = jnp.where(lane == j, rank_j, rank_out)
    idx_ref[...] = idx_out.astype(jnp.int32)
    wt_ref[...] = wt_out
    rank_ref[...] = rank_out.astype(jnp.int32)

    run_ref[...] = run_ref[...] + jnp.sum(chosen, axis=0, keepdims=True)
    cnt_ref[...] = run_ref[...].astype(jnp.int32)


def _out_proj(y_ssd, y_gla, x2, wo1, wo2, norm_w, wr_hi, wr_lo, rb, stri, *, tm):
    m = x2.shape[0]
    const = lambda shape: pl.BlockSpec(shape, lambda i: (0,) * len(shape))
    tile = lambda w: pl.BlockSpec((tm, w), lambda i: (i, 0))
    return pl.pallas_call(
        _out_proj_kernel,
        grid=(m // tm,),
        in_specs=[
            tile(SSD_WIDTH), tile(GLA_VALUE_WIDTH), tile(D_MODEL),
            const((SSD_WIDTH, D_MODEL)), const((GLA_VALUE_WIDTH, D_MODEL)),
            const((1, D_MODEL)),
            const((D_MODEL, LANES)), const((D_MODEL, LANES)), const((1, LANES)),
            const((tm, tm)),
        ],
        out_specs=[
            tile(D_MODEL), tile(D_MODEL), tile(LANES), tile(LANES), tile(LANES),
            const((SUBLANES, LANES)),
        ],
        out_shape=[
            jax.ShapeDtypeStruct((m, D_MODEL), F32),
            jax.ShapeDtypeStruct((m, D_MODEL), F32),
            jax.ShapeDtypeStruct((m, LANES), jnp.int32),
            jax.ShapeDtypeStruct((m, LANES), F32),
            jax.ShapeDtypeStruct((m, LANES), jnp.int32),
            jax.ShapeDtypeStruct((SUBLANES, LANES), jnp.int32),
        ],
        scratch_shapes=[pltpu.VMEM((SUBLANES, LANES), F32)],
        compiler_params=pltpu.CompilerParams(
            dimension_semantics=("arbitrary",),
            vmem_limit_bytes=VMEM_LIMIT_BYTES),
        name="out_proj",
    )(y_ssd, y_gla, x2, wo1, wo2, norm_w, wr_hi, wr_lo, rb, stri)


def _dispatch_kernel(dest_ref, n2_hbm, xg_in_hbm, xg_hbm, sem):
    del xg_in_hbm
    tm = TOKEN_TILE
    base = pl.program_id(0) * tm

    def copy(r, j):
        return pltpu.make_async_copy(
            n2_hbm.at[pl.ds(base + r, 1)],
            xg_hbm.at[pl.ds(dest_ref[0, 0, TOP_K * r + j], 1)],
            sem)

    def start(r, carry):
        for j in range(TOP_K):
            copy(r, j).start()
        return carry

    def wait(r, carry):
        for j in range(TOP_K):
            copy(r, j).wait()
        return carry

    lax.fori_loop(0, tm, start, 0)
    lax.fori_loop(0, tm, wait, 0)


def _dispatch(dest3, n2, xg_zero):
    n_tiles = dest3.shape[0]
    return pl.pallas_call(
        _dispatch_kernel,
        grid=(n_tiles,),
        in_specs=[
            pl.BlockSpec((1, 1, TOP_K * TOKEN_TILE), lambda i: (i, 0, 0),
                         memory_space=pltpu.SMEM),
            pl.BlockSpec(memory_space=pl.ANY),
            pl.BlockSpec(memory_space=pl.ANY),
        ],
        out_specs=pl.BlockSpec(memory_space=pl.ANY),
        out_shape=jax.ShapeDtypeStruct(xg_zero.shape, xg_zero.dtype),
        scratch_shapes=[pltpu.SemaphoreType.DMA(())],
        input_output_aliases={2: 0},
        compiler_params=pltpu.CompilerParams(
            dimension_semantics=("arbitrary",), has_side_effects=True),
        name="dispatch",
    )(dest3, n2, xg_zero)


def _ffn_kernel(sbe_ref, nact_ref, x_ref, wg_ref, wu_ref, wd_ref, bg_ref, bu_ref, bd_ref,
                o_ref, xb_ref):
    del sbe_ref
    i = pl.program_id(0)
    f = pl.program_id(1)
    last_f = pl.num_programs(1) - 1
    nact = nact_ref[i]
    for sb in range(SUB_PER_SUPER):
        rows = slice(sb * ROW_BLOCK, (sb + 1) * ROW_BLOCK)

        @pl.when(sb < nact)
        def _():
            @pl.when(f == 0)
            def _():
                xb_ref[rows, :] = x_ref[rows, :].astype(BF16)

            xs = xb_ref[rows, :]
            gate = jnp.minimum(_dot(xs, wg_ref[0]) + bg_ref[0], SWIGLU_LIMIT)
            up = jnp.clip(_dot(xs, wu_ref[0]) + bu_ref[0], -SWIGLU_LIMIT, SWIGLU_LIMIT)
            act = (up + 1.0) * gate * jax.nn.sigmoid(SWIGLU_ALPHA * gate)
            part = _dot(act.astype(BF16), wd_ref[0])

            @pl.when(f == 0)
            def _():
                o_ref[rows, :] = part + bd_ref[0]

            @pl.when(f > 0)
            def _():
                o_ref[rows, :] = o_ref[rows, :] + part

        @pl.when(jnp.logical_and(sb >= nact, f == last_f))
        def _():
            o_ref[rows, :] = jnp.zeros((ROW_BLOCK, D_MODEL), F32)


def _ffn(sb_expert, sb_nact, xg, wgu, wd, bgu3, bd3):
    n_super = sb_expert.shape[0]
    nf = D_FF // FFN_TILE
    grid_spec = pltpu.PrefetchScalarGridSpec(
        num_scalar_prefetch=2,
        grid=(n_super, nf),
        in_specs=[
            pl.BlockSpec((SUPER_ROWS, D_MODEL), lambda i, f, e, n: (i, 0)),
            pl.BlockSpec((1, D_MODEL, FFN_TILE), lambda i, f, e, n: (e[i], 0, f)),
            pl.BlockSpec((1, D_MODEL, FFN_TILE), lambda i, f, e, n: (e[i], 0, f + nf)),
            pl.BlockSpec((1, FFN_TILE, D_MODEL), lambda i, f, e, n: (e[i], f, 0)),
            pl.BlockSpec((1, 1, FFN_TILE), lambda i, f, e, n: (e[i], 0, f)),
            pl.BlockSpec((1, 1, FFN_TILE), lambda i, f, e, n: (e[i], 0, f + nf)),
            pl.BlockSpec((1, 1, D_MODEL), lambda i, f, e, n: (e[i], 0, 0)),
        ],
        out_specs=pl.BlockSpec((SUPER_ROWS, D_MODEL), lambda i, f, e, n: (i, 0)),
        scratch_shapes=[pltpu.VMEM((SUPER_ROWS, D_MODEL), BF16)],
    )
    return pl.pallas_call(
        _ffn_kernel,
        grid_spec=grid_spec,
        out_shape=jax.ShapeDtypeStruct(xg.shape, F32),
        compiler_params=pltpu.CompilerParams(
            dimension_semantics=("arbitrary", "arbitrary"),
            vmem_limit_bytes=VMEM_LIMIT_BYTES),
        name="ffn",
    )(sb_expert, sb_nact, xg, wgu, wgu, wd, bgu3, bgu3, bd3)


def _combine_kernel(dcur_ref, dnxt_ref, wt_ref, h1_ref, fw_ref, eo_hbm, o_ref, buf, sem):
    tm = TOKEN_TILE
    i = pl.program_id(0)
    n = pl.num_programs(0)
    slot = lax.rem(i, 2)

    def copy(dref, s, r, j):
        return pltpu.make_async_copy(
            eo_hbm.at[pl.ds(dref[0, 0, TOP_K * r + j], 1)],
            buf.at[s, j, pl.ds(r, 1)],
            sem.at[s])

    def start_all(dref, s):
        def body(r, carry):
            for j in range(TOP_K):
                copy(dref, s, r, j).start()
            return carry
        lax.fori_loop(0, tm, body, 0)

    @pl.when(i == 0)
    def _():
        start_all(dcur_ref, 0)

    @pl.when(i + 1 < n)
    def _():
        start_all(dnxt_ref, 1 - slot)

    def wait_body(r, carry):
        for j in range(TOP_K):
            copy(dcur_ref, slot, r, j).wait()
        return carry
    lax.fori_loop(0, tm, wait_body, 0)

    wt = wt_ref[...]
    y = h1_ref[...]
    for j in range(TOP_K):
        y = y + buf[slot, j] * wt[:, j:j + 1]
    ms = jnp.mean(y * y, axis=-1, keepdims=True)
    o_ref[...] = y * lax.rsqrt(ms + NORM_EPS) * fw_ref[...]


def _combine(dest3, wt, h1, final_w, eo):
    n_tiles = dest3.shape[0]
    tm = TOKEN_TILE
    m = h1.shape[0]
    smem = lambda fn: pl.BlockSpec((1, 1, TOP_K * tm), fn, memory_space=pltpu.SMEM)
    return pl.pallas_call(
        _combine_kernel,
        grid=(n_tiles,),
        in_specs=[
            smem(lambda i: (i, 0, 0)),
            smem(lambda i: (jnp.minimum(i + 1, n_tiles - 1), 0, 0)),
            pl.BlockSpec((tm, LANES), lambda i: (i, 0)),
            pl.BlockSpec((tm, D_MODEL), lambda i: (i, 0)),
            pl.BlockSpec((1, D_MODEL), lambda i: (0, 0)),
            pl.BlockSpec(memory_space=pl.ANY),
        ],
        out_specs=pl.BlockSpec((tm, D_MODEL), lambda i: (i, 0)),
        out_shape=jax.ShapeDtypeStruct((m, D_MODEL), F32),
        scratch_shapes=[
            pltpu.VMEM((2, TOP_K, tm, D_MODEL), F32),
            pltpu.SemaphoreType.DMA((2,)),
        ],
        compiler_params=pltpu.CompilerParams(
            dimension_semantics=("arbitrary",),
            vmem_limit_bytes=VMEM_LIMIT_BYTES),
        name="combine",
    )(dest3, dest3, wt, h1, final_w, eo)


def _pad_lanes(v, width=LANES):
    return jnp.pad(v, [(0, 0)] * (v.ndim - 1) + [(0, width - v.shape[-1])])


def _layer(h, norm_mix_w, in_proj_w, conv_w, conv_b, dt_bias, a_log, d_skip, ssd_norm_w,
           gate_up_w, gate_up_b, gla_norm_w, out_proj_w, norm_ffn_w, router_w, router_b,
           w_gate_up, b_gate_up, w_down, b_down, out_norm_w):
    batch, seq, _ = h.shape
    m = batch * seq
    x2 = h.reshape(m, D_MODEL)
    L = MIX_CHUNK

    w = in_proj_w
    o_xbc = SSD_WIDTH
    o_dt = o_xbc + SSD_CONV_DIM
    o_q = o_dt + SSD_HEADS
    o_k = o_q + GLA_KEY_WIDTH
    o_v = o_k + GLA_KEY_WIDTH
    o_g = o_v + GLA_VALUE_WIDTH
    o_gl = o_g + GLA_VALUE_WIDTH
    w_main = jnp.concatenate([
        w[:, 0:SSD_WIDTH], w[:, o_xbc:o_xbc + SSD_WIDTH], w[:, o_v:o_g], w[:, o_g:o_gl],
        w[:, o_xbc + SSD_WIDTH:o_dt], w[:, o_q:o_k], w[:, o_k:o_v]], axis=1).astype(BF16)
    w_small = _pad_lanes(jnp.concatenate([w[:, o_dt:o_q], w[:, o_gl:o_gl + GLA_GATE_RANK]],
                                         axis=1))
    ws_hi, ws_lo = _split2(w_small)

    u, small = _in_proj(x2, norm_mix_w.reshape(1, D_MODEL), w_main, ws_hi, ws_lo,
                        tm=min(1024, m), tn=1024)

    tri = jnp.tril(jnp.ones((L, L), F32)).astype(BF16)
    head_of_col = jnp.arange(SSD_WIDTH) // SSD_HEAD_DIM
    expand = (jnp.arange(LANES)[:, None] == head_of_col[None, :]).astype(BF16)
    e2 = jnp.concatenate([expand, expand], axis=0)
    y_ssd = _ssd(u, small, conv_w, conv_b.reshape(1, -1),
                 _pad_lanes(dt_bias.reshape(1, -1)), _pad_lanes(a_log.reshape(1, -1)),
                 jnp.repeat(d_skip, SSD_HEAD_DIM).reshape(1, -1), ssd_norm_w.reshape(1, -1),
                 e2, tri, batch=batch, seq=seq)

    wup = jnp.zeros((SMALL_WIDTH, GLA_KEY_WIDTH), F32)
    wup = wup.at[SMALL_GATE_OFF:SMALL_GATE_OFF + GLA_GATE_RANK].set(gate_up_w)
    wup_hi, wup_lo = _split2(wup)
    y_gla = _gla(u, small, wup_hi, wup_lo, gate_up_b.reshape(1, -1), gla_norm_w.reshape(1, -1),
                 tri, batch=batch, seq=seq)

    tm = TOKEN_TILE
    wr_hi, wr_lo = _split2(_pad_lanes(router_w))
    stri = jnp.tril(jnp.ones((tm, tm), F32), k=-1).astype(BF16)
    wo = out_proj_w.astype(BF16)
    h1, n2, top_idx, top_w, rank, cnt = _out_proj(
        y_ssd, y_gla, x2, wo[:SSD_WIDTH], wo[SSD_WIDTH:], norm_ffn_w.reshape(1, -1),
        wr_hi, wr_lo, _pad_lanes(router_b.reshape(1, -1)), stri, tm=tm)

    counts = cnt[0, :N_EXPERTS]
    n_sb = (counts + SUPER_ROWS - 1) // SUPER_ROWS
    sb_end = jnp.cumsum(n_sb)
    sb_start = sb_end - n_sb
    dest = (sb_start * SUPER_ROWS)[top_idx[:, :TOP_K]] + rank[:, :TOP_K]
    dest3 = dest.astype(jnp.int32).reshape(m // tm, 1, TOP_K * tm)

    n_super = (m * TOP_K + N_EXPERTS * (SUPER_ROWS - 1)) // SUPER_ROWS
    sb_ids = jnp.arange(n_super)
    total_sb = sb_end[-1]
    sb_live = jnp.minimum(sb_ids, total_sb - 1)
    sb_expert = jnp.minimum(jnp.searchsorted(sb_end, sb_live, side="right"), N_EXPERTS - 1)
    rows_left = counts[sb_expert] - (sb_live - sb_start[sb_expert]) * SUPER_ROWS
    sb_nact = jnp.where(sb_ids < total_sb,
                        (jnp.clip(rows_left, 0, SUPER_ROWS) + ROW_BLOCK - 1) // ROW_BLOCK, 0)

    xg = _dispatch(dest3, n2, jnp.zeros((n_super * SUPER_ROWS, D_MODEL), F32))
    eo = _ffn(sb_expert.astype(jnp.int32), sb_nact.astype(jnp.int32), xg,
              w_gate_up.astype(BF16), w_down.astype(BF16),
              b_gate_up.reshape(N_EXPERTS, 1, 2 * D_FF), b_down.reshape(N_EXPERTS, 1, D_MODEL))
    out = _combine(dest3, top_w, h1, out_norm_w.reshape(1, -1), eo)
    return out.reshape(batch, seq, D_MODEL)


def kernel(x, norm_mix_w, in_proj_w, conv_w, conv_b, dt_bias, a_log, d_skip, ssd_norm_w,
           gate_up_w, gate_up_b, gla_norm_w, out_proj_w, norm_ffn_w, router_w, router_b,
           w_gate_up, b_gate_up, w_down, b_down, final_norm_w):
    assert norm_mix_w.shape[0] == 1
    return _layer(x, norm_mix_w[0], in_proj_w[0], conv_w[0], conv_b[0], dt_bias[0], a_log[0],
                  d_skip[0], ssd_norm_w[0], gate_up_w[0], gate_up_b[0], gla_norm_w[0],
                  out_proj_w[0], norm_ffn_w[0], router_w[0], router_b[0], w_gate_up[0],
                  b_gate_up[0], w_down[0], b_down[0], final_norm_w)
```

```python
import functools

import jax
import jax.numpy as jnp
from jax import lax
from jax.experimental import pallas as pl
from jax.experimental.pallas import tpu as pltpu

F32 = jnp.float32
BF16 = jnp.bfloat16

D_MODEL = 2048
NORM_EPS = 1e-5

SSD_WIDTH = 2048
SSD_HEAD_DIM = 64
SSD_HEADS = 32
SSD_GROUPS = 4
SSD_STATE = 128
CONV_WIDTH = 4
SSD_GROUP_WIDTH = SSD_WIDTH // SSD_GROUPS
SSD_BC_WIDTH = 2 * SSD_GROUPS * SSD_STATE
SSD_CONV_DIM = SSD_WIDTH + SSD_BC_WIDTH

GLA_HEADS = 4
GLA_KEY_WIDTH = 1024
GLA_VALUE_WIDTH = 2048
GLA_HEAD_K = 256
GLA_HEAD_V = 512
GLA_GATE_RANK = 16
GLA_GATE_NORMALIZER = 16.0

N_EXPERTS = 32
TOP_K = 4
D_FF = 2048
SWIGLU_LIMIT = 7.0
SWIGLU_ALPHA = 1.702

LANES = 128
SUBLANES = 8
VMEM_LIMIT_BYTES = 56 * 1024 * 1024

U_WIDTH = 11264
SMALL_WIDTH = LANES
SMALL_DT_OFF = 0
SMALL_GATE_OFF = SSD_HEADS

MIX_CHUNK = 128
GLA_SUB = 16
GLA_EXP_CLAMP = 80.0

ROW_BLOCK = 256
SUPER_ROWS = 1024
SUB_PER_SUPER = SUPER_ROWS // ROW_BLOCK
FFN_TILE = 512
TOKEN_TILE = 256


def _split2(v):
    hi = v.astype(BF16)
    lo = (v - hi.astype(F32)).astype(BF16)
    return hi, lo


def _split3(v):
    hi = v.astype(BF16)
    r = v - hi.astype(F32)
    mid = r.astype(BF16)
    lo = (r - mid.astype(F32)).astype(BF16)
    return hi, mid, lo


def _dot(a, b):
    return jnp.dot(a, b, preferred_element_type=F32)


def _dot_nt(a, b):
    return lax.dot_general(a, b, (((1,), (1,)), ((), ())), preferred_element_type=F32)


def _dot_exact_lhs(mat01, v):
    hi, mid, lo = _split3(v)
    return _dot(mat01, hi) + _dot(mat01, mid) + _dot(mat01, lo)


def _dot_hilo(a, b_hi, b_lo):
    a_hi, a_lo = _split2(a)
    return _dot(a_hi, b_hi) + _dot(a_lo, b_hi) + _dot(a_hi, b_lo)


def _silu(v):
    return v * jax.nn.sigmoid(v)


def _softplus(v):
    return jnp.maximum(v, 0.0) + jnp.log1p(jnp.exp(-jnp.abs(v)))


def _log_sigmoid(v):
    return jnp.minimum(v, 0.0) - jnp.log1p(jnp.exp(-jnp.abs(v)))


def _in_proj_kernel(x_ref, nw_ref, w_ref, wsh_ref, wsl_ref, u_ref, small_ref, n1_ref):
    @pl.when(pl.program_id(1) == 0)
    def _():
        x = x_ref[...]
        ms = jnp.mean(x * x, axis=-1, keepdims=True)
        n1 = x * lax.rsqrt(ms + NORM_EPS) * nw_ref[...]
        n1_ref[...] = n1.astype(BF16)
        small_ref[...] = _dot_hilo(n1, wsh_ref[...], wsl_ref[...])

    u_ref[...] = _dot(n1_ref[...], w_ref[...]).astype(BF16)


def _in_proj(x2, norm_w, w_main, ws_hi, ws_lo, *, tm, tn):
    m = x2.shape[0]
    grid = (m // tm, U_WIDTH // tn)
    return pl.pallas_call(
        _in_proj_kernel,
        grid=grid,
        in_specs=[
            pl.BlockSpec((tm, D_MODEL), lambda i, j: (i, 0)),
            pl.BlockSpec((1, D_MODEL), lambda i, j: (0, 0)),
            pl.BlockSpec((D_MODEL, tn), lambda i, j: (0, j)),
            pl.BlockSpec((D_MODEL, SMALL_WIDTH), lambda i, j: (0, 0)),
            pl.BlockSpec((D_MODEL, SMALL_WIDTH), lambda i, j: (0, 0)),
        ],
        out_specs=[
            pl.BlockSpec((tm, tn), lambda i, j: (i, j)),
            pl.BlockSpec((tm, SMALL_WIDTH), lambda i, j: (i, 0)),
        ],
        out_shape=[
            jax.ShapeDtypeStruct((m, U_WIDTH), BF16),
            jax.ShapeDtypeStruct((m, SMALL_WIDTH), F32),
        ],
        scratch_shapes=[pltpu.VMEM((tm, D_MODEL), BF16)],
        compiler_params=pltpu.CompilerParams(
            dimension_semantics=("arbitrary", "arbitrary"),
            vmem_limit_bytes=VMEM_LIMIT_BYTES),
        name="in_proj",
    )(x2, norm_w, w_main, ws_hi, ws_lo)


def _ssd_kernel(z_ref, x_ref, bc_ref, sm_ref, convw_ref, convb_ref, dtb_ref, alog_ref,
                dsk_ref, nw_ref, e2_ref, tri_ref, o_ref,
                ext_ref, xc_ref, xd_ref, xdd_ref, y_ref, state_ref):
    L = MIX_CHUNK
    G, N, P = SSD_GROUPS, SSD_STATE, SSD_HEAD_DIM
    GW = SSD_GROUP_WIDTH

    @pl.when(pl.program_id(1) == 0)
    def _():
        ext_ref[0:SUBLANES, :] = jnp.zeros((SUBLANES, SSD_CONV_DIM), F32)
        state_ref[...] = jnp.zeros(state_ref.shape, F32)

    ext_ref[SUBLANES:SUBLANES + L, 0:SSD_WIDTH] = x_ref[...].astype(F32)
    ext_ref[SUBLANES:SUBLANES + L, SSD_WIDTH:SSD_CONV_DIM] = bc_ref[...].astype(F32)
    slab = 512
    for s in range(SSD_CONV_DIM // slab):
        cols = slice(s * slab, (s + 1) * slab)
        acc = jnp.broadcast_to(convb_ref[:, cols], (L, slab))
        for w in range(CONV_WIDTH):
            off = SUBLANES - (CONV_WIDTH - 1) + w
            acc = acc + convw_ref[w:w + 1, cols] * ext_ref[off:off + L, cols]
        xc_ref[:, cols] = _silu(acc)
    ext_ref[0:SUBLANES, :] = ext_ref[L:L + SUBLANES, :]

    lane = lax.broadcasted_iota(jnp.int32, (1, LANES), 1)
    a = jnp.where(lane < SSD_HEADS, -jnp.exp(alog_ref[...]), 0.0)
    dt = _softplus(sm_ref[...] + dtb_ref[...])
    la = _dot_exact_lhs(tri_ref[...], dt * a)
    la_t = la.T
    la_end = la[L - 1:L, :]

    def expand(v):
        hi, lo = _split2(v)
        return _dot(jnp.concatenate([hi, lo], axis=1), e2_ref[...])

    dt_e = expand(dt)
    ela_e = expand(jnp.exp(la))
    dte_e = expand(jnp.exp(la_end - la))
    cd_e = expand(jnp.broadcast_to(jnp.exp(la_end), (SUBLANES, LANES)))[0:1, :]

    for s in range(SSD_WIDTH // slab):
        cols = slice(s * slab, (s + 1) * slab)
        xd = xc_ref[:, cols] * dt_e[:, cols]
        xd_ref[:, cols] = xd
        xdd_ref[:, cols] = (xd * dte_e[:, cols]).astype(BF16)

    row = lax.broadcasted_iota(jnp.int32, (L, L), 0)
    col = lax.broadcasted_iota(jnp.int32, (L, L), 1)
    causal = row >= col
    lane_l = lax.broadcasted_iota(jnp.int32, (L, LANES), 1)

    for g in range(G):
        bm = xc_ref[:, SSD_WIDTH + g * N:SSD_WIDTH + (g + 1) * N]
        cm = xc_ref[:, SSD_WIDTH + G * N + g * N:SSD_WIDTH + G * N + (g + 1) * N]
        cm_b = cm.astype(BF16)
        cb = _dot_nt(cm_b, bm.astype(BF16))
        gcols = slice(g * GW, (g + 1) * GW)
        y_off = _dot(cm_b, state_ref[g].astype(BF16)) * ela_e[:, gcols]
        for p in range(GW // LANES):
            h0 = g * (GW // P) + 2 * p
            ms = []
            for h in (h0, h0 + 1):
                seg = la[:, h:h + 1] - la_t[h:h + 1, :]
                dec = jnp.where(causal, jnp.exp(jnp.minimum(seg, 0.0)), 0.0)
                ms.append((cb * dec).astype(BF16))
            pcols = slice(g * GW + p * LANES, g * GW + (p + 1) * LANES)
            xp = xd_ref[:, pcols]
            rhs = jnp.concatenate([jnp.where(lane_l < P, xp, 0.0),
                                   jnp.where(lane_l >= P, xp, 0.0)], axis=0).astype(BF16)
            y_pair = _dot(jnp.concatenate(ms, axis=1), rhs)
            y_ref[:, pcols] = (y_pair + y_off[:, p * LANES:(p + 1) * LANES]
                               + dsk_ref[:, pcols] * xc_ref[:, pcols])
        upd = _dot(bm.T.astype(BF16), xdd_ref[:, gcols])
        state_ref[g] = cd_e[:, gcols] * state_ref[g] + upd

    for g in range(G):
        gcols = slice(g * GW, (g + 1) * GW)
        y = y_ref[:, gcols] * _silu(z_ref[:, gcols].astype(F32))
        ms = jnp.mean(y * y, axis=-1, keepdims=True)
        o_ref[:, gcols] = (y * lax.rsqrt(ms + NORM_EPS) * nw_ref[:, gcols]).astype(BF16)


def _ssd(u, small, conv_w, conv_b, dtb, alog, dsk_e, norm_w, e2, tri, *, batch, seq):
    L = MIX_CHUNK
    nc = seq // L
    m = batch * seq

    def rows(b, c):
        return b * nc + c

    const = lambda shape: pl.BlockSpec(shape, lambda b, c: (0,) * len(shape))
    return pl.pallas_call(
        _ssd_kernel,
        grid=(batch, nc),
        in_specs=[
            pl.BlockSpec((L, SSD_WIDTH), lambda b, c: (rows(b, c), 0)),
            pl.BlockSpec((L, SSD_WIDTH), lambda b, c: (rows(b, c), 1)),
            pl.BlockSpec((L, SSD_BC_WIDTH), lambda b, c: (rows(b, c), 8)),
            pl.BlockSpec((L, SMALL_WIDTH), lambda b, c: (rows(b, c), 0)),
            const((CONV_WIDTH, SSD_CONV_DIM)),
            const((1, SSD_CONV_DIM)),
            const((1, LANES)),
            const((1, LANES)),
            const((1, SSD_WIDTH)),
            const((1, SSD_WIDTH)),
            const((2 * LANES, SSD_WIDTH)),
            const((L, L)),
        ],
        out_specs=pl.BlockSpec((L, SSD_WIDTH), lambda b, c: (rows(b, c), 0)),
        out_shape=jax.ShapeDtypeStruct((m, SSD_WIDTH), BF16),
        scratch_shapes=[
            pltpu.VMEM((L + 2 * SUBLANES, SSD_CONV_DIM), F32),
            pltpu.VMEM((L, SSD_CONV_DIM), F32),
            pltpu.VMEM((L, SSD_WIDTH), F32),
            pltpu.VMEM((L, SSD_WIDTH), BF16),
            pltpu.VMEM((L, SSD_WIDTH), F32),
            pltpu.VMEM((SSD_GROUPS, SSD_STATE, SSD_GROUP_WIDTH), F32),
        ],
        compiler_params=pltpu.CompilerParams(
            dimension_semantics=("arbitrary", "arbitrary"),
            vmem_limit_bytes=VMEM_LIMIT_BYTES),
        name="ssd",
    )(u, u, u, small, conv_w, conv_b, dtb, alog, dsk_e, norm_w, e2, tri)


def _gla_kernel(q_ref, k_ref, v_ref, g_ref, sm_ref, wuh_ref, wul_ref, bup_ref, nw_ref,
                tri_ref, o_ref, cum_ref, s_ref, st_ref):
    L, C = MIX_CHUNK, GLA_SUB
    K, V = GLA_HEAD_K, GLA_HEAD_V

    @pl.when(pl.program_id(1) == 0)
    def _():
        st_ref[...] = jnp.zeros(st_ref.shape, F32)

    pre = _dot_hilo(sm_ref[...], wuh_ref[...], wul_ref[...]) + bup_ref[...]
    gk = _log_sigmoid(pre) * (1.0 / GLA_GATE_NORMALIZER)
    cum_ref[...] = _dot_exact_lhs(tri_ref[...], gk)

    row_c = lax.broadcasted_iota(jnp.int32, (C, L), 0)
    col_c = lax.broadcasted_iota(jnp.int32, (C, L), 1)
    for h in range(GLA_HEADS):
        ks = slice(h * K, (h + 1) * K)
        vs = slice(h * V, (h + 1) * V)
        cum = cum_ref[:, ks]
        cum_last = cum_ref[L - 1:L, ks]
        qh = q_ref[:, ks].astype(F32) * (K ** -0.5)
        kh = k_ref[:, ks].astype(F32)
        vh = v_ref[:, vs]
        for blk in range(L // C):
            rows = slice(blk * C, (blk + 1) * C)
            if blk == 0:
                ref_row = jnp.zeros((1, K), F32)
            else:
                ref_row = cum_ref[blk * C - 1:blk * C, ks]
            q_b = (qh[rows] * jnp.exp(cum_ref[rows, ks] - ref_row)).astype(BF16)
            k_b = (kh * jnp.exp(jnp.minimum(ref_row - cum, GLA_EXP_CLAMP))).astype(BF16)
            s_b = _dot_nt(q_b, k_b)
            s_ref[rows, :] = jnp.where(col_c <= row_c + blk * C, s_b, 0.0)
        o_intra = _dot(s_ref[...].astype(BF16), vh)
        st = st_ref[h]
        o_inter = _dot_nt((qh * jnp.exp(cum)).astype(BF16), st.astype(BF16))
        k_dec = (kh * jnp.exp(cum_last - cum)).astype(BF16)
        st_ref[h] = st * jnp.exp(cum_last) + _dot(vh.astype(F32).T.astype(BF16), k_dec)
        o = o_inter + o_intra
        ms = jnp.mean(o * o, axis=-1, keepdims=True)
        o = o * lax.rsqrt(ms + NORM_EPS) * nw_ref[...] * _silu(g_ref[:, vs].astype(F32))
        o_ref[:, vs] = o.astype(BF16)


def _gla(u, small, wup_hi, wup_lo, bup, norm_w, tri, *, batch, seq):
    L = MIX_CHUNK
    nc = seq // L
    m = batch * seq

    def rows(b, c):
        return b * nc + c

    const = lambda shape: pl.BlockSpec(shape, lambda b, c: (0,) * len(shape))
    return pl.pallas_call(
        _gla_kernel,
        grid=(batch, nc),
        in_specs=[
            pl.BlockSpec((L, GLA_KEY_WIDTH), lambda b, c: (rows(b, c), 9)),
            pl.BlockSpec((L, GLA_KEY_WIDTH), lambda b, c: (rows(b, c), 10)),
            pl.BlockSpec((L, GLA_VALUE_WIDTH), lambda b, c: (rows(b, c), 2)),
            pl.BlockSpec((L, GLA_VALUE_WIDTH), lambda b, c: (rows(b, c), 3)),
            pl.BlockSpec((L, SMALL_WIDTH), lambda b, c: (rows(b, c), 0)),
            const((SMALL_WIDTH, GLA_KEY_WIDTH)),
            const((SMALL_WIDTH, GLA_KEY_WIDTH)),
            const((1, GLA_KEY_WIDTH)),
            const((1, GLA_HEAD_V)),
            const((L, L)),
        ],
        out_specs=pl.BlockSpec((L, GLA_VALUE_WIDTH), lambda b, c: (rows(b, c), 0)),
        out_shape=jax.ShapeDtypeStruct((m, GLA_VALUE_WIDTH), BF16),
        scratch_shapes=[
            pltpu.VMEM((L, GLA_KEY_WIDTH), F32),
            pltpu.VMEM((L, L), F32),
            pltpu.VMEM((GLA_HEADS, GLA_HEAD_V, GLA_HEAD_K), F32),
        ],
        compiler_params=pltpu.CompilerParams(
            dimension_semantics=("arbitrary", "arbitrary"),
            vmem_limit_bytes=VMEM_LIMIT_BYTES),
        name="gla",
    )(u, u, u, u, small, wup_hi, wup_lo, bup, norm_w, tri)


def _out_proj_kernel(ys_ref, yg_ref, x_ref, wo1_ref, wo2_ref, nw_ref, wrh_ref, wrl_ref,
                     rb_ref, stri_ref, h1_ref, n2_ref, idx_ref, wt_ref, rank_ref, cnt_ref,
                     run_ref):
    tm = x_ref.shape[0]

    @pl.when(pl.program_id(0) == 0)
    def _():
        run_ref[...] = jnp.zeros(run_ref.shape, F32)

    h = x_ref[...] + _dot(ys_ref[...], wo1_ref[...]) + _dot(yg_ref[...], wo2_ref[...])
    h1_ref[...] = h
    ms = jnp.mean(h * h, axis=-1, keepdims=True)
    n2 = h * lax.rsqrt(ms + NORM_EPS) * nw_ref[...]
    n2_ref[...] = n2

    lane = lax.broadcasted_iota(jnp.int32, (tm, LANES), 1)
    lane_f = lane.astype(F32)
    logits = _dot_hilo(n2, wrh_ref[...], wrl_ref[...]) + rb_ref[...]
    work = jnp.where(lane < N_EXPERTS, logits, -jnp.inf)
    vals, idxs, sels = [], [], []
    for _ in range(TOP_K):
        top = jnp.max(work, axis=-1, keepdims=True)
        idx = jnp.min(jnp.where(work == top, lane_f, float(LANES)), axis=-1, keepdims=True)
        sel = lane_f == idx
        vals.append(top)
        idxs.append(idx)
        sels.append(sel)
        work = jnp.where(sel, -jnp.inf, work)
    exps = [jnp.exp(v - vals[0]) for v in vals]
    denom = exps[0] + exps[1] + exps[2] + exps[3]

    chosen = jnp.zeros((tm, LANES), F32)
    for sel in sels:
        chosen = chosen + jnp.where(sel, 1.0, 0.0)
    before = run_ref[0:1, :] + _dot(stri_ref[...], chosen.astype(BF16))

    idx_out = jnp.zeros((tm, LANES), F32)
    wt_out = jnp.zeros((tm, LANES), F32)
    rank_out = jnp.zeros((tm, LANES), F32)
    for j in range(TOP_K):
        rank_j = jnp.sum(jnp.where(sels[j], before, 0.0), axis=-1, keepdims=True)
        idx_out = jnp.where(lane == j, idxs[j], idx_out)
        wt_out = jnp.where(lane == j, exps[j] / denom, wt_out)
        rank_out = jnp.where(lane == j, rank_j, rank_out)
    idx_ref[...] = idx_out.astype(jnp.int32)
    wt_ref[...] = wt_out
    rank_ref[...] = rank_out.astype(jnp.int32)

    run_ref[...] = run_ref[...] + jnp.sum(chosen, axis=0, keepdims=True)
    cnt_ref[...] = run_ref[...].astype(jnp.int32)


def _out_proj(y_ssd, y_gla, x2, wo1, wo2, norm_w, wr_hi, wr_lo, rb, stri, *, tm):
    m = x2.shape[0]
    const = lambda shape: pl.BlockSpec(shape, lambda i: (0,) * len(shape))
    tile = lambda w: pl.BlockSpec((tm, w), lambda i: (i, 0))
    return pl.pallas_call(
        _out_proj_kernel,
        grid=(m // tm,),
        in_specs=[
            tile(SSD_WIDTH), tile(GLA_VALUE_WIDTH), tile(D_MODEL),
            const((SSD_WIDTH, D_MODEL)), const((GLA_VALUE_WIDTH, D_MODEL)),
            const((1, D_MODEL)),
            const((D_MODEL, LANES)), const((D_MODEL, LANES)), const((1, LANES)),
            const((tm, tm)),
        ],
        out_specs=[
            tile(D_MODEL), tile(D_MODEL), tile(LANES), tile(LANES), tile(LANES),
            const((SUBLANES, LANES)),
        ],
        out_shape=[
            jax.ShapeDtypeStruct((m, D_MODEL), F32),
            jax.ShapeDtypeStruct((m, D_MODEL), F32),
            jax.ShapeDtypeStruct((m, LANES), jnp.int32),
            jax.ShapeDtypeStruct((m, LANES), F32),
            jax.ShapeDtypeStruct((m, LANES), jnp.int32),
            jax.ShapeDtypeStruct((SUBLANES, LANES), jnp.int32),
        ],
        scratch_shapes=[pltpu.VMEM((SUBLANES, LANES), F32)],
        compiler_params=pltpu.CompilerParams(
            dimension_semantics=("arbitrary",),
            vmem_limit_bytes=VMEM_LIMIT_BYTES),
        name="out_proj",
    )(y_ssd, y_gla, x2, wo1, wo2, norm_w, wr_hi, wr_lo, rb, stri)


def _dispatch_kernel(dest_ref, n2_ref, xg_in_hbm, xg_hbm, sem):
    del xg_in_hbm
    tm = TOKEN_TILE

    def copy(r, j):
        return pltpu.make_async_copy(
            n2_ref.at[pl.ds(r, 1)],
            xg_hbm.at[pl.ds(dest_ref[0, 0, TOP_K * r + j], 1)],
            sem)

    def start(r, carry):
        for j in range(TOP_K):
            copy(r, j).start()
        return carry

    def wait(r, carry):
        for j in range(TOP_K):
            copy(r, j).wait()
        return carry

    lax.fori_loop(0, tm, start, 0)
    lax.fori_loop(0, tm, wait, 0)


def _dispatch(dest3, n2, xg_zero):
    n_tiles = dest3.shape[0]
    return pl.pallas_call(
        _dispatch_kernel,
        grid=(n_tiles,),
        in_specs=[
            pl.BlockSpec((1, 1, TOP_K * TOKEN_TILE), lambda i: (i, 0, 0),
                         memory_space=pltpu.SMEM),
            pl.BlockSpec((TOKEN_TILE, D_MODEL), lambda i: (i, 0)),
            pl.BlockSpec(memory_space=pl.ANY),
        ],
        out_specs=pl.BlockSpec(memory_space=pl.ANY),
        out_shape=jax.ShapeDtypeStruct(xg_zero.shape, xg_zero.dtype),
        scratch_shapes=[pltpu.SemaphoreType.DMA(())],
        input_output_aliases={2: 0},
        compiler_params=pltpu.CompilerParams(
            dimension_semantics=("arbitrary",), has_side_effects=True),
        name="dispatch",
    )(dest3, n2, xg_zero)


def _ffn_kernel(sbe_ref, nact_ref, x_ref, wg_ref, wu_ref, wd_ref, bg_ref, bu_ref, bd_ref,
                o_ref, xb_ref):
    del sbe_ref
    i = pl.program_id(0)
    f = pl.program_id(1)
    last_f = pl.num_programs(1) - 1
    nact = nact_ref[i]
    for sb in range(SUB_PER_SUPER):
        rows = slice(sb * ROW_BLOCK, (sb + 1) * ROW_BLOCK)

        @pl.when(sb < nact)
        def _():
            @pl.when(f == 0)
            def _():
                xb_ref[rows, :] = x_ref[rows, :].astype(BF16)

            xs = xb_ref[rows, :]
            gate = jnp.minimum(_dot(xs, wg_ref[0]) + bg_ref[0], SWIGLU_LIMIT)
            up = jnp.clip(_dot(xs, wu_ref[0]) + bu_ref[0], -SWIGLU_LIMIT, SWIGLU_LIMIT)
            act = (up + 1.0) * gate * jax.nn.sigmoid(SWIGLU_ALPHA * gate)
            part = _dot(act.astype(BF16), wd_ref[0])

            @pl.when(f == 0)
            def _():
                o_ref[rows, :] = part + bd_ref[0]

            @pl.when(f > 0)
            def _():
                o_ref[rows, :] = o_ref[rows, :] + part

        @pl.when(jnp.logical_and(sb >= nact, f == last_f))
        def _():
            o_ref[rows, :] = jnp.zeros((ROW_BLOCK, D_MODEL), F32)


def _ffn(sb_expert, sb_nact, xg, wgu, wd, bgu3, bd3):
    n_super = sb_expert.shape[0]
    nf = D_FF // FFN_TILE
    grid_spec = pltpu.PrefetchScalarGridSpec(
        num_scalar_prefetch=2,
        grid=(n_super, nf),
        in_specs=[
            pl.BlockSpec((SUPER_ROWS, D_MODEL), lambda i, f, e, n: (i, 0)),
            pl.BlockSpec((1, D_MODEL, FFN_TILE), lambda i, f, e, n: (e[i], 0, f)),
            pl.BlockSpec((1, D_MODEL, FFN_TILE), lambda i, f, e, n: (e[i], 0, f + nf)),
            pl.BlockSpec((1, FFN_TILE, D_MODEL), lambda i, f, e, n: (e[i], f, 0)),
            pl.BlockSpec((1, 1, FFN_TILE), lambda i, f, e, n: (e[i], 0, f)),
            pl.BlockSpec((1, 1, FFN_TILE), lambda i, f, e, n: (e[i], 0, f + nf)),
            pl.BlockSpec((1, 1, D_MODEL), lambda i, f, e, n: (e[i], 0, 0)),
        ],
        out_specs=pl.BlockSpec((SUPER_ROWS, D_MODEL), lambda i, f, e, n: (i, 0)),
        scratch_shapes=[pltpu.VMEM((SUPER_ROWS, D_MODEL), BF16)],
    )
    return pl.pallas_call(
        _ffn_kernel,
        grid_spec=grid_spec,
        out_shape=jax.ShapeDtypeStruct(xg.shape, F32),
        compiler_params=pltpu.CompilerParams(
            dimension_semantics=("arbitrary", "arbitrary"),
            vmem_limit_bytes=VMEM_LIMIT_BYTES),
        name="ffn",
    )(sb_expert, sb_nact, xg, wgu, wgu, wd, bgu3, bgu3, bd3)


def _combine_kernel(dcur_ref, dnxt_ref, wt_ref, h1_ref, fw_ref, eo_hbm, o_ref, buf, sem):
    tm = TOKEN_TILE
    i = pl.program_id(0)
    n = pl.num_programs(0)
    slot = lax.rem(i, 2)

    def copy(dref, s, r, j):
        return pltpu.make_async_copy(
            eo_hbm.at[pl.ds(dref[0, 0, TOP_K * r + j], 1)],
            buf.at[s, j, pl.ds(r, 1)],
            sem.at[s])

    def start_all(dref, s):
        def body(r, carry):
            for j in range(TOP_K):
                copy(dref, s, r, j).start()
            return carry
        lax.fori_loop(0, tm, body, 0)

    @pl.when(i == 0)
    def _():
        start_all(dcur_ref, 0)

    @pl.when(i + 1 < n)
    def _():
        start_all(dnxt_ref, 1 - slot)

    def wait_body(r, carry):
        for j in range(TOP_K):
            copy(dcur_ref, slot, r, j).wait()
        return carry
    lax.fori_loop(0, tm, wait_body, 0)

    wt = wt_ref[...]
    y = h1_ref[...]
    for j in range(TOP_K):
        y = y + buf[slot, j] * wt[:, j:j + 1]
    ms = jnp.mean(y * y, axis=-1, keepdims=True)
    o_ref[...] = y * lax.rsqrt(ms + NORM_EPS) * fw_ref[...]


def _combine(dest3, wt, h1, final_w, eo):
    n_tiles = dest3.shape[0]
    tm = TOKEN_TILE
    m = h1.shape[0]
    smem = lambda fn: pl.BlockSpec((1, 1, TOP_K * tm), fn, memory_space=pltpu.SMEM)
    return pl.pallas_call(
        _combine_kernel,
        grid=(n_tiles,),
        in_specs=[
            smem(lambda i: (i, 0, 0)),
            smem(lambda i: (jnp.minimum(i + 1, n_tiles - 1), 0, 0)),
            pl.BlockSpec((tm, LANES), lambda i: (i, 0)),
            pl.BlockSpec((tm, D_MODEL), lambda i: (i, 0)),
            pl.BlockSpec((1, D_MODEL), lambda i: (0, 0)),
            pl.BlockSpec(memory_space=pl.ANY),
        ],
        out_specs=pl.BlockSpec((tm, D_MODEL), lambda i: (i, 0)),
        out_shape=jax.ShapeDtypeStruct((m, D_MODEL), F32),
        scratch_shapes=[
            pltpu.VMEM((2, TOP_K, tm, D_MODEL), F32),
            pltpu.SemaphoreType.DMA((2,)),
        ],
        compiler_params=pltpu.CompilerParams(
            dimension_semantics=("arbitrary",),
            vmem_limit_bytes=VMEM_LIMIT_BYTES),
        name="combine",
    )(dest3, dest3, wt, h1, final_w, eo)


def _pad_lanes(v, width=LANES):
    return jnp.pad(v, [(0, 0)] * (v.ndim - 1) + [(0, width - v.shape[-1])])


def _layer(h, norm_mix_w, in_proj_w, conv_w, conv_b, dt_bias, a_log, d_skip, ssd_norm_w,
           gate_up_w, gate_up_b, gla_norm_w, out_proj_w, norm_ffn_w, router_w, router_b,
           w_gate_up, b_gate_up, w_down, b_down, out_norm_w):
    batch, seq, _ = h.shape
    m = batch * seq
    x2 = h.reshape(m, D_MODEL)
    L = MIX_CHUNK

    w = in_proj_w
    o_xbc = SSD_WIDTH
    o_dt = o_xbc + SSD_CONV_DIM
    o_q = o_dt + SSD_HEADS
    o_k = o_q + GLA_KEY_WIDTH
    o_v = o_k + GLA_KEY_WIDTH
    o_g = o_v + GLA_VALUE_WIDTH
    o_gl = o_g + GLA_VALUE_WIDTH
    w_main = jnp.concatenate([
        w[:, 0:SSD_WIDTH], w[:, o_xbc:o_xbc + SSD_WIDTH], w[:, o_v:o_g], w[:, o_g:o_gl],
        w[:, o_xbc + SSD_WIDTH:o_dt], w[:, o_q:o_k], w[:, o_k:o_v]], axis=1).astype(BF16)
    w_small = _pad_lanes(jnp.concatenate([w[:, o_dt:o_q], w[:, o_gl:o_gl + GLA_GATE_RANK]],
                                         axis=1))
    ws_hi, ws_lo = _split2(w_small)

    u, small = _in_proj(x2, norm_mix_w.reshape(1, D_MODEL), w_main, ws_hi, ws_lo,
                        tm=min(1024, m), tn=1024)

    tri = jnp.tril(jnp.ones((L, L), F32)).astype(BF16)
    head_of_col = jnp.arange(SSD_WIDTH) // SSD_HEAD_DIM
    expand = (jnp.arange(LANES)[:, None] == head_of_col[None, :]).astype(BF16)
    e2 = jnp.concatenate([expand, expand], axis=0)
    y_ssd = _ssd(u, small, conv_w, conv_b.reshape(1, -1),
                 _pad_lanes(dt_bias.reshape(1, -1)), _pad_lanes(a_log.reshape(1, -1)),
                 jnp.repeat(d_skip, SSD_HEAD_DIM).reshape(1, -1), ssd_norm_w.reshape(1, -1),
                 e2, tri, batch=batch, seq=seq)

    wup = jnp.zeros((SMALL_WIDTH, GLA_KEY_WIDTH), F32)
    wup = wup.at[SMALL_GATE_OFF:SMALL_GATE_OFF + GLA_GATE_RANK].set(gate_up_w)
    wup_hi, wup_lo = _split2(wup)
    y_gla = _gla(u, small, wup_hi, wup_lo, gate_up_b.reshape(1, -1), gla_norm_w.reshape(1, -1),
                 tri, batch=batch, seq=seq)

    tm = TOKEN_TILE
    wr_hi, wr_lo = _split2(_pad_lanes(router_w))
    stri = jnp.tril(jnp.ones((tm, tm), F32), k=-1).astype(BF16)
    wo = out_proj_w.astype(BF16)
    h1, n2, top_idx, top_w, rank, cnt = _out_proj(
        y_ssd, y_gla, x2, wo[:SSD_WIDTH], wo[SSD_WIDTH:], norm_ffn_w.reshape(1, -1),
        wr_hi, wr_lo, _pad_lanes(router_b.reshape(1, -1)), stri, tm=tm)

    counts = cnt[0, :N_EXPERTS]
    n_sb = (counts + SUPER_ROWS - 1) // SUPER_ROWS
    sb_end = jnp.cumsum(n_sb)
    sb_start = sb_end - n_sb
    dest = (sb_start * SUPER_ROWS)[top_idx[:, :TOP_K]] + rank[:, :TOP_K]
    dest3 = dest.astype(jnp.int32).reshape(m // tm, 1, TOP_K * tm)

    n_super = (m * TOP_K + N_EXPERTS * (SUPER_ROWS - 1)) // SUPER_ROWS
    sb_ids = jnp.arange(n_super)
    total_sb = sb_end[-1]
    sb_live = jnp.minimum(sb_ids, total_sb - 1)
    sb_expert = jnp.minimum(jnp.searchsorted(sb_end, sb_live, side="right"), N_EXPERTS - 1)
    rows_left = counts[sb_expert] - (sb_live - sb_start[sb_expert]) * SUPER_ROWS
    sb_nact = jnp.where(sb_ids < total_sb,
                        (jnp.clip(rows_left, 0, SUPER_ROWS) + ROW_BLOCK - 1) // ROW_BLOCK, 0)

    xg = _dispatch(dest3, n2, jnp.zeros((n_super * SUPER_ROWS, D_MODEL), F32))
    eo = _ffn(sb_expert.astype(jnp.int32), sb_nact.astype(jnp.int32), xg,
              w_gate_up.astype(BF16), w_down.astype(BF16),
              b_gate_up.reshape(N_EXPERTS, 1, 2 * D_FF), b_down.reshape(N_EXPERTS, 1, D_MODEL))
    out = _combine(dest3, top_w, h1, out_norm_w.reshape(1, -1), eo)
    return out.reshape(batch, seq, D_MODEL)


def kernel(x, norm_mix_w, in_proj_w, conv_w, conv_b, dt_bias, a_log, d_skip, ssd_norm_w,
           gate_up_w, gate_up_b, gla_norm_w, out_proj_w, norm_ffn_w, router_w, router_b,
           w_gate_up, b_gate_up, w_down, b_down, final_norm_w):
    assert norm_mix_w.shape[0] == 1
    return _layer(x, norm_mix_w[0], in_proj_w[0], conv_w[0], conv_b[0], dt_bias[0], a_log[0],
                  d_skip[0], ssd_norm_w[0], gate_up_w[0], gate_up_b[0], gla_norm_w[0],
                  out_proj_w[0], norm_ffn_w[0], router_w[0], router_b[0], w_gate_up[0],
                  b_gate_up[0], w_down[0], b_down[0], final_norm_w)
```

```python
import functools

import jax
import jax.numpy as jnp
from jax import lax
from jax.experimental import pallas as pl
from jax.experimental.pallas import tpu as pltpu

F32 = jnp.float32
BF16 = jnp.bfloat16

D_MODEL = 2048
NORM_EPS = 1e-5

SSD_WIDTH = 2048
SSD_HEAD_DIM = 64
SSD_HEADS = 32
SSD_GROUPS = 4
SSD_STATE = 128
CONV_WIDTH = 4
SSD_GROUP_WIDTH = SSD_WIDTH // SSD_GROUPS
SSD_BC_WIDTH = 2 * SSD_GROUPS * SSD_STATE
SSD_CONV_DIM = SSD_WIDTH + SSD_BC_WIDTH

GLA_HEADS = 4
GLA_KEY_WIDTH = 1024
GLA_VALUE_WIDTH = 2048
GLA_HEAD_K = 256
GLA_HEAD_V = 512
GLA_GATE_RANK = 16
GLA_GATE_NORMALIZER = 16.0

N_EXPERTS = 32
TOP_K = 4
D_FF = 2048
SWIGLU_LIMIT = 7.0
SWIGLU_ALPHA = 1.702

LANES = 128
SUBLANES = 8
VMEM_LIMIT_BYTES = 56 * 1024 * 1024

U_WIDTH = 11264
SMALL_WIDTH = LANES
SMALL_DT_OFF = 0
SMALL_GATE_OFF = SSD_HEADS

MIX_CHUNK = 128
GLA_SUB = 16
GLA_EXP_CLAMP = 80.0

ROW_BLOCK = 256
SUPER_ROWS = 1024
SUB_PER_SUPER = SUPER_ROWS // ROW_BLOCK
PACKED_WIDTH = D_MODEL // 2
FFN_TILE = 512
TOKEN_TILE = 256


def _split2(v):
    hi = v.astype(BF16)
    lo = (v - hi.astype(F32)).astype(BF16)
    return hi, lo


def _split3(v):
    hi = v.astype(BF16)
    r = v - hi.astype(F32)
    mid = r.astype(BF16)
    lo = (r - mid.astype(F32)).astype(BF16)
    return hi, mid, lo


def _dot(a, b):
    return jnp.dot(a, b, preferred_element_type=F32)


def _dot_nt(a, b):
    return lax.dot_general(a, b, (((1,), (1,)), ((), ())), preferred_element_type=F32)


def _dot_exact_lhs(mat01, v):
    hi, mid, lo = _split3(v)
    return _dot(mat01, hi) + _dot(mat01, mid) + _dot(mat01, lo)


def _dot_hilo(a, b_hi, b_lo):
    a_hi, a_lo = _split2(a)
    return _dot(a_hi, b_hi) + _dot(a_lo, b_hi) + _dot(a_hi, b_lo)


def _pack_bf16_pairs(v):
    c = v.shape[1] // 2
    lo = lax.bitcast_convert_type(v[:, :c].astype(BF16).astype(F32), jnp.uint32)
    hi = lax.bitcast_convert_type(v[:, c:].astype(BF16).astype(F32), jnp.uint32)
    return (lo >> 16) | (hi & jnp.uint32(0xFFFF0000))


def _unpack_bf16_pairs(w):
    lo = lax.bitcast_convert_type(w << 16, F32)
    hi = lax.bitcast_convert_type(w & jnp.uint32(0xFFFF0000), F32)
    return lo, hi


def _silu(v):
    return v * jax.nn.sigmoid(v)


def _softplus(v):
    return jnp.maximum(v, 0.0) + jnp.log1p(jnp.exp(-jnp.abs(v)))


def _log_sigmoid(v):
    return jnp.minimum(v, 0.0) - jnp.log1p(jnp.exp(-jnp.abs(v)))


def _in_proj_kernel(x_ref, nw_ref, w_ref, wsh_ref, wsl_ref, u_ref, small_ref, n1_ref):
    @pl.when(pl.program_id(1) == 0)
    def _():
        x = x_ref[...]
        ms = jnp.mean(x * x, axis=-1, keepdims=True)
        n1 = x * lax.rsqrt(ms + NORM_EPS) * nw_ref[...]
        n1_ref[...] = n1.astype(BF16)
        small_ref[...] = _dot_hilo(n1, wsh_ref[...], wsl_ref[...])

    u_ref[...] = _dot(n1_ref[...], w_ref[...]).astype(BF16)


def _in_proj(x2, norm_w, w_main, ws_hi, ws_lo, *, tm, tn):
    m = x2.shape[0]
    grid = (m // tm, U_WIDTH // tn)
    return pl.pallas_call(
        _in_proj_kernel,
        grid=grid,
        in_specs=[
            pl.BlockSpec((tm, D_MODEL), lambda i, j: (i, 0)),
            pl.BlockSpec((1, D_MODEL), lambda i, j: (0, 0)),
            pl.BlockSpec((D_MODEL, tn), lambda i, j: (0, j)),
            pl.BlockSpec((D_MODEL, SMALL_WIDTH), lambda i, j: (0, 0)),
            pl.BlockSpec((D_MODEL, SMALL_WIDTH), lambda i, j: (0, 0)),
        ],
        out_specs=[
            pl.BlockSpec((tm, tn), lambda i, j: (i, j)),
            pl.BlockSpec((tm, SMALL_WIDTH), lambda i, j: (i, 0)),
        ],
        out_shape=[
            jax.ShapeDtypeStruct((m, U_WIDTH), BF16),
            jax.ShapeDtypeStruct((m, SMALL_WIDTH), F32),
        ],
        scratch_shapes=[pltpu.VMEM((tm, D_MODEL), BF16)],
        compiler_params=pltpu.CompilerParams(
            dimension_semantics=("arbitrary", "arbitrary"),
            vmem_limit_bytes=VMEM_LIMIT_BYTES),
        name="in_proj",
    )(x2, norm_w, w_main, ws_hi, ws_lo)


def _ssd_kernel(z_ref, x_ref, bc_ref, sm_ref, convw_ref, convb_ref, dtb_ref, alog_ref,
                dsk_ref, nw_ref, e2_ref, tri_ref, o_ref,
                ext_ref, xc_ref, xd_ref, xdd_ref, y_ref, state_ref):
    L = MIX_CHUNK
    G, N, P = SSD_GROUPS, SSD_STATE, SSD_HEAD_DIM
    GW = SSD_GROUP_WIDTH

    @pl.when(pl.program_id(1) == 0)
    def _():
        ext_ref[0:SUBLANES, :] = jnp.zeros((SUBLANES, SSD_CONV_DIM), F32)
        state_ref[...] = jnp.zeros(state_ref.shape, F32)

    ext_ref[SUBLANES:SUBLANES + L, 0:SSD_WIDTH] = x_ref[...].astype(F32)
    ext_ref[SUBLANES:SUBLANES + L, SSD_WIDTH:SSD_CONV_DIM] = bc_ref[...].astype(F32)
    slab = 512
    for s in range(SSD_CONV_DIM // slab):
        cols = slice(s * slab, (s + 1) * slab)
        acc = jnp.broadcast_to(convb_ref[:, cols], (L, slab))
        for w in range(CONV_WIDTH):
            off = SUBLANES - (CONV_WIDTH - 1) + w
            acc = acc + convw_ref[w:w + 1, cols] * ext_ref[off:off + L, cols]
        xc_ref[:, cols] = _silu(acc)
    ext_ref[0:SUBLANES, :] = ext_ref[L:L + SUBLANES, :]

    lane = lax.broadcasted_iota(jnp.int32, (1, LANES), 1)
    a = jnp.where(lane < SSD_HEADS, -jnp.exp(alog_ref[...]), 0.0)
    dt = _softplus(sm_ref[...] + dtb_ref[...])
    la = _dot_exact_lhs(tri_ref[...], dt * a)
    la_t = la.T
    la_end = la[L - 1:L, :]

    def expand(v):
        hi, lo = _split2(v)
        return _dot(jnp.concatenate([hi, lo], axis=1), e2_ref[...])

    dt_e = expand(dt)
    ela_e = expand(jnp.exp(la))
    dte_e = expand(jnp.exp(la_end - la))
    cd_e = expand(jnp.broadcast_to(jnp.exp(la_end), (SUBLANES, LANES)))[0:1, :]

    for s in range(SSD_WIDTH // slab):
        cols = slice(s * slab, (s + 1) * slab)
        xd = xc_ref[:, cols] * dt_e[:, cols]
        xd_ref[:, cols] = xd
        xdd_ref[:, cols] = (xd * dte_e[:, cols]).astype(BF16)

    row = lax.broadcasted_iota(jnp.int32, (L, L), 0)
    col = lax.broadcasted_iota(jnp.int32, (L, L), 1)
    causal = row >= col
    lane_l = lax.broadcasted_iota(jnp.int32, (L, LANES), 1)

    for g in range(G):
        bm = xc_ref[:, SSD_WIDTH + g * N:SSD_WIDTH + (g + 1) * N]
        cm = xc_ref[:, SSD_WIDTH + G * N + g * N:SSD_WIDTH + G * N + (g + 1) * N]
        cm_b = cm.astype(BF16)
        cb = _dot_nt(cm_b, bm.astype(BF16))
        gcols = slice(g * GW, (g + 1) * GW)
        y_off = _dot(cm_b, state_ref[g].astype(BF16)) * ela_e[:, gcols]
        for p in range(GW // LANES):
            h0 = g * (GW // P) + 2 * p
            ms = []
            for h in (h0, h0 + 1):
                seg = la[:, h:h + 1] - la_t[h:h + 1, :]
                dec = jnp.where(causal, jnp.exp(jnp.minimum(seg, 0.0)), 0.0)
                ms.append((cb * dec).astype(BF16))
            pcols = slice(g * GW + p * LANES, g * GW + (p + 1) * LANES)
            xp = xd_ref[:, pcols]
            rhs = jnp.concatenate([jnp.where(lane_l < P, xp, 0.0),
                                   jnp.where(lane_l >= P, xp, 0.0)], axis=0).astype(BF16)
            y_pair = _dot(jnp.concatenate(ms, axis=1), rhs)
            y_ref[:, pcols] = (y_pair + y_off[:, p * LANES:(p + 1) * LANES]
                               + dsk_ref[:, pcols] * xc_ref[:, pcols])
        upd = _dot(bm.T.astype(BF16), xdd_ref[:, gcols])
        state_ref[g] = cd_e[:, gcols] * state_ref[g] + upd

    for g in range(G):
        gcols = slice(g * GW, (g + 1) * GW)
        y = y_ref[:, gcols] * _silu(z_ref[:, gcols].astype(F32))
        ms = jnp.mean(y * y, axis=-1, keepdims=True)
        o_ref[:, gcols] = (y * lax.rsqrt(ms + NORM_EPS) * nw_ref[:, gcols]).astype(BF16)


def _ssd(u, small, conv_w, conv_b, dtb, alog, dsk_e, norm_w, e2, tri, *, batch, seq):
    L = MIX_CHUNK
    nc = seq // L
    m = batch * seq

    def rows(b, c):
        return b * nc + c

    const = lambda shape: pl.BlockSpec(shape, lambda b, c: (0,) * len(shape))
    return pl.pallas_call(
        _ssd_kernel,
        grid=(batch, nc),
        in_specs=[
            pl.BlockSpec((L, SSD_WIDTH), lambda b, c: (rows(b, c), 0)),
            pl.BlockSpec((L, SSD_WIDTH), lambda b, c: (rows(b, c), 1)),
            pl.BlockSpec((L, SSD_BC_WIDTH), lambda b, c: (rows(b, c), 8)),
            pl.BlockSpec((L, SMALL_WIDTH), lambda b, c: (rows(b, c), 0)),
            const((CONV_WIDTH, SSD_CONV_DIM)),
            const((1, SSD_CONV_DIM)),
            const((1, LANES)),
            const((1, LANES)),
            const((1, SSD_WIDTH)),
            const((1, SSD_WIDTH)),
            const((2 * LANES, SSD_WIDTH)),
            const((L, L)),
        ],
        out_specs=pl.BlockSpec((L, SSD_WIDTH), lambda b, c: (rows(b, c), 0)),
        out_shape=jax.ShapeDtypeStruct((m, SSD_WIDTH), BF16),
        scratch_shapes=[
            pltpu.VMEM((L + 2 * SUBLANES, SSD_CONV_DIM), F32),
            pltpu.VMEM((L, SSD_CONV_DIM), F32),
            pltpu.VMEM((L, SSD_WIDTH), F32),
            pltpu.VMEM((L, SSD_WIDTH), BF16),
            pltpu.VMEM((L, SSD_WIDTH), F32),
            pltpu.VMEM((SSD_GROUPS, SSD_STATE, SSD_GROUP_WIDTH), F32),
        ],
        compiler_params=pltpu.CompilerParams(
            dimension_semantics=("arbitrary", "arbitrary"),
            vmem_limit_bytes=VMEM_LIMIT_BYTES),
        name="ssd",
    )(u, u, u, small, conv_w, conv_b, dtb, alog, dsk_e, norm_w, e2, tri)


def _gla_kernel(q_ref, k_ref, v_ref, g_ref, sm_ref, wuh_ref, wul_ref, bup_ref, nw_ref,
                tri_ref, o_ref, cum_ref, s_ref, st_ref):
    L, C = MIX_CHUNK, GLA_SUB
    K, V = GLA_HEAD_K, GLA_HEAD_V

    @pl.when(pl.program_id(1) == 0)
    def _():
        st_ref[...] = jnp.zeros(st_ref.shape, F32)

    pre = _dot_hilo(sm_ref[...], wuh_ref[...], wul_ref[...]) + bup_ref[...]
    gk = _log_sigmoid(pre) * (1.0 / GLA_GATE_NORMALIZER)
    cum_ref[...] = _dot_exact_lhs(tri_ref[...], gk)

    row_c = lax.broadcasted_iota(jnp.int32, (C, L), 0)
    col_c = lax.broadcasted_iota(jnp.int32, (C, L), 1)
    for h in range(GLA_HEADS):
        ks = slice(h * K, (h + 1) * K)
        vs = slice(h * V, (h + 1) * V)
        cum = cum_ref[:, ks]
        cum_last = cum_ref[L - 1:L, ks]
        qh = q_ref[:, ks].astype(F32) * (K ** -0.5)
        kh = k_ref[:, ks].astype(F32)
        vh = v_ref[:, vs]
        for blk in range(L // C):
            rows = slice(blk * C, (blk + 1) * C)
            if blk == 0:
                ref_row = jnp.zeros((1, K), F32)
            else:
                ref_row = cum_ref[blk * C - 1:blk * C, ks]
            q_b = (qh[rows] * jnp.exp(cum_ref[rows, ks] - ref_row)).astype(BF16)
            k_b = (kh * jnp.exp(jnp.minimum(ref_row - cum, GLA_EXP_CLAMP))).astype(BF16)
            s_b = _dot_nt(q_b, k_b)
            s_ref[rows, :] = jnp.where(col_c <= row_c + blk * C, s_b, 0.0)
        o_intra = _dot(s_ref[...].astype(BF16), vh)
        st = st_ref[h]
        o_inter = _dot_nt((qh * jnp.exp(cum)).astype(BF16), st.astype(BF16))
        k_dec = (kh * jnp.exp(cum_last - cum)).astype(BF16)
        st_ref[h] = st * jnp.exp(cum_last) + _dot(vh.astype(F32).T.astype(BF16), k_dec)
        o = o_inter + o_intra
        ms = jnp.mean(o * o, axis=-1, keepdims=True)
        o = o * lax.rsqrt(ms + NORM_EPS) * nw_ref[...] * _silu(g_ref[:, vs].astype(F32))
        o_ref[:, vs] = o.astype(BF16)


def _gla(u, small, wup_hi, wup_lo, bup, norm_w, tri, *, batch, seq):
    L = MIX_CHUNK
    nc = seq // L
    m = batch * seq

    def rows(b, c):
        return b * nc + c

    const = lambda shape: pl.BlockSpec(shape, lambda b, c: (0,) * len(shape))
    return pl.pallas_call(
        _gla_kernel,
        grid=(batch, nc),
        in_specs=[
            pl.BlockSpec((L, GLA_KEY_WIDTH), lambda b, c: (rows(b, c), 9)),
            pl.BlockSpec((L, GLA_KEY_WIDTH), lambda b, c: (rows(b, c), 10)),
            pl.BlockSpec((L, GLA_VALUE_WIDTH), lambda b, c: (rows(b, c), 2)),
            pl.BlockSpec((L, GLA_VALUE_WIDTH), lambda b, c: (rows(b, c), 3)),
            pl.BlockSpec((L, SMALL_WIDTH), lambda b, c: (rows(b, c), 0)),
            const((SMALL_WIDTH, GLA_KEY_WIDTH)),
            const((SMALL_WIDTH, GLA_KEY_WIDTH)),
            const((1, GLA_KEY_WIDTH)),
            const((1, GLA_HEAD_V)),
            const((L, L)),
        ],
        out_specs=pl.BlockSpec((L, GLA_VALUE_WIDTH), lambda b, c: (rows(b, c), 0)),
        out_shape=jax.ShapeDtypeStruct((m, GLA_VALUE_WIDTH), BF16),
        scratch_shapes=[
            pltpu.VMEM((L, GLA_KEY_WIDTH), F32),
            pltpu.VMEM((L, L), F32),
            pltpu.VMEM((GLA_HEADS, GLA_HEAD_V, GLA_HEAD_K), F32),
        ],
        compiler_params=pltpu.CompilerParams(
            dimension_semantics=("arbitrary", "arbitrary"),
            vmem_limit_bytes=VMEM_LIMIT_BYTES),
        name="gla",
    )(u, u, u, u, small, wup_hi, wup_lo, bup, norm_w, tri)


def _out_proj_kernel(ys_ref, yg_ref, x_ref, wo1_ref, wo2_ref, nw_ref, wrh_ref, wrl_ref,
                     rb_ref, stri_ref, h1_ref, n2_ref, idx_ref, wt_ref, rank_ref, cnt_ref,
                     run_ref):
    tm = x_ref.shape[0]

    @pl.when(pl.program_id(0) == 0)
    def _():
        run_ref[...] = jnp.zeros(run_ref.shape, F32)

    h = x_ref[...] + _dot(ys_ref[...], wo1_ref[...]) + _dot(yg_ref[...], wo2_ref[...])
    h1_ref[...] = h
    ms = jnp.mean(h * h, axis=-1, keepdims=True)
    n2 = h * lax.rsqrt(ms + NORM_EPS) * nw_ref[...]
    n2_ref[...] = _pack_bf16_pairs(n2)

    lane = lax.broadcasted_iota(jnp.int32, (tm, LANES), 1)
    lane_f = lane.astype(F32)
    logits = _dot_hilo(n2, wrh_ref[...], wrl_ref[...]) + rb_ref[...]
    work = jnp.where(lane < N_EXPERTS, logits, -jnp.inf)
    vals, idxs, sels = [], [], []
    for _ in range(TOP_K):
        top = jnp.max(work, axis=-1, keepdims=True)
        idx = jnp.min(jnp.where(work == top, lane_f, float(LANES)), axis=-1, keepdims=True)
        sel = lane_f == idx
        vals.append(top)
        idxs.append(idx)
        sels.append(sel)
        work = jnp.where(sel, -jnp.inf, work)
    exps = [jnp.exp(v - vals[0]) for v in vals]
    denom = exps[0] + exps[1] + exps[2] + exps[3]

    chosen = jnp.zeros((tm, LANES), F32)
    for sel in sels:
        chosen = chosen + jnp.where(sel, 1.0, 0.0)
    before = run_ref[0:1, :] + _dot(stri_ref[...], chosen.astype(BF16))

    idx_out = jnp.zeros((tm, LANES), F32)
    wt_out = jnp.zeros((tm, LANES), F32)
    rank_out = jnp.zeros((tm, LANES), F32)
    for j in range(TOP_K):
        rank_j = jnp.sum(jnp.where(sels[j], before, 0.0), axis=-1, keepdims=True)
        idx_out = jnp.where(lane == j, idxs[j], idx_out)
        wt_out = jnp.where(lane == j, exps[j] / denom, wt_out)
        rank_out = jnp.where(lane == j, rank_j, rank_out)
    idx_ref[...] = idx_out.astype(jnp.int32)
    wt_ref[...] = wt_out
    rank_ref[...] = rank_out.astype(jnp.int32)

    run_ref[...] = run_ref[...] + jnp.sum(chosen, axis=0, keepdims=True)
    cnt_ref[...] = run_ref[...].astype(jnp.int32)


def _out_proj(y_ssd, y_gla, x2, wo1, wo2, norm_w, wr_hi, wr_lo, rb, stri, *, tm):
    m = x2.shape[0]
    const = lambda shape: pl.BlockSpec(shape, lambda i: (0,) * len(shape))
    tile = lambda w: pl.BlockSpec((tm, w), lambda i: (i, 0))
    return pl.pallas_call(
        _out_proj_kernel,
        grid=(m // tm,),
        in_specs=[
            tile(SSD_WIDTH), tile(GLA_VALUE_WIDTH), tile(D_MODEL),
            const((SSD_WIDTH, D_MODEL)), const((GLA_VALUE_WIDTH, D_MODEL)),
            const((1, D_MODEL)),
            const((D_MODEL, LANES)), const((D_MODEL, LANES)), const((1, LANES)),
            const((tm, tm)),
        ],
        out_specs=[
            tile(D_MODEL), tile(PACKED_WIDTH), tile(LANES), tile(LANES), tile(LANES),
            const((SUBLANES, LANES)),
        ],
        out_shape=[
            jax.ShapeDtypeStruct((m, D_MODEL), F32),
            jax.ShapeDtypeStruct((m, PACKED_WIDTH), jnp.uint32),
            jax.ShapeDtypeStruct((m, LANES), jnp.int32),
            jax.ShapeDtypeStruct((m, LANES), F32),
            jax.ShapeDtypeStruct((m, LANES), jnp.int32),
            jax.ShapeDtypeStruct((SUBLANES, LANES), jnp.int32),
        ],
        scratch_shapes=[pltpu.VMEM((SUBLANES, LANES), F32)],
        compiler_params=pltpu.CompilerParams(
            dimension_semantics=("arbitrary",),
            vmem_limit_bytes=VMEM_LIMIT_BYTES),
        name="out_proj",
    )(y_ssd, y_gla, x2, wo1, wo2, norm_w, wr_hi, wr_lo, rb, stri)


def _dispatch_kernel(fill_ref, dest_ref, n2_ref, xg_hbm, zero_ref, sem, zsem):
    tm = TOKEN_TILE

    @pl.when(pl.program_id(0) == 0)
    def _():
        zero_ref[...] = jnp.zeros(zero_ref.shape, zero_ref.dtype)

        def fill(b):
            row = pl.multiple_of(b * ROW_BLOCK, ROW_BLOCK)
            return pltpu.make_async_copy(zero_ref, xg_hbm.at[pl.ds(row, ROW_BLOCK)], zsem)

        def start_fill(b, carry):
            @pl.when(fill_ref[b] > 0)
            def _():
                fill(b).start()
            return carry

        def wait_fill(b, carry):
            @pl.when(fill_ref[b] > 0)
            def _():
                fill(b).wait()
            return carry

        lax.fori_loop(0, fill_ref.shape[0], start_fill, 0)
        lax.fori_loop(0, fill_ref.shape[0], wait_fill, 0)

    def copy(r, j):
        return pltpu.make_async_copy(
            n2_ref.at[pl.ds(r, 1)],
            xg_hbm.at[pl.ds(dest_ref[0, 0, TOP_K * r + j], 1)],
            sem)

    def start(r, carry):
        for j in range(TOP_K):
            copy(r, j).start(priority=j % 2)
        return carry

    def wait(r, carry):
        for j in range(TOP_K):
            copy(r, j).wait()
        return carry

    lax.fori_loop(0, tm, start, 0)
    lax.fori_loop(0, tm, wait, 0)


def _dispatch(fill, dest3, n2p):
    n_tiles = dest3.shape[0]
    n_rows = fill.shape[0] * ROW_BLOCK
    grid_spec = pltpu.PrefetchScalarGridSpec(
        num_scalar_prefetch=1,
        grid=(n_tiles,),
        in_specs=[
            pl.BlockSpec((1, 1, TOP_K * TOKEN_TILE), lambda i, fl: (i, 0, 0),
                         memory_space=pltpu.SMEM),
            pl.BlockSpec((TOKEN_TILE, PACKED_WIDTH), lambda i, fl: (i, 0)),
        ],
        out_specs=pl.BlockSpec(memory_space=pl.ANY),
        scratch_shapes=[
            pltpu.VMEM((ROW_BLOCK, PACKED_WIDTH), jnp.uint32),
            pltpu.SemaphoreType.DMA(()),
            pltpu.SemaphoreType.DMA(()),
        ],
    )
    return pl.pallas_call(
        _dispatch_kernel,
        grid_spec=grid_spec,
        out_shape=jax.ShapeDtypeStruct((n_rows, PACKED_WIDTH), jnp.uint32),
        compiler_params=pltpu.CompilerParams(
            dimension_semantics=("arbitrary",), has_side_effects=True),
        name="dispatch",
    )(fill, dest3, n2p)


def _ffn_kernel(sbe_ref, nact_ref, x_ref, wg_ref, wu_ref, wd_ref, bg_ref, bu_ref, bd_ref,
                o_ref, xb_ref, act_ref):
    del sbe_ref
    i = pl.program_id(0)
    f = pl.program_id(1)
    nf = D_FF // FFN_TILE
    nact = nact_ref[i]
    for sb in range(SUB_PER_SUPER):
        rows = slice(sb * ROW_BLOCK, (sb + 1) * ROW_BLOCK)

        @pl.when(sb < nact)
        def _():
            @pl.when(f == 0)
            def _():
                lo, hi = _unpack_bf16_pairs(x_ref[rows, :])
                xb_ref[rows, 0:PACKED_WIDTH] = lo.astype(BF16)
                xb_ref[rows, PACKED_WIDTH:D_MODEL] = hi.astype(BF16)

            xs = xb_ref[rows, :]
            gate = jnp.minimum(_dot(xs, wg_ref[0]) + bg_ref[0], SWIGLU_LIMIT)
            up = jnp.clip(_dot(xs, wu_ref[0]) + bu_ref[0], -SWIGLU_LIMIT, SWIGLU_LIMIT)
            act = ((up + 1.0) * gate * jax.nn.sigmoid(SWIGLU_ALPHA * gate)).astype(BF16)
            for k in range(nf):
                @pl.when(f == k)
                def _():
                    act_ref[rows, k * FFN_TILE:(k + 1) * FFN_TILE] = act

            @pl.when(f == nf - 1)
            def _():
                out = _dot(act_ref[rows, :], wd_ref[0]) + bd_ref[0]
                o_ref[rows, :] = _pack_bf16_pairs(out)

        @pl.when(jnp.logical_and(sb >= nact, f == nf - 1))
        def _():
            o_ref[rows, :] = jnp.zeros((ROW_BLOCK, PACKED_WIDTH), jnp.uint32)


def _ffn(sb_expert, sb_nact, xg, wgu, wd, bgu3, bd3):
    n_super = sb_expert.shape[0]
    nf = D_FF // FFN_TILE
    grid_spec = pltpu.PrefetchScalarGridSpec(
        num_scalar_prefetch=2,
        grid=(n_super, nf),
        in_specs=[
            pl.BlockSpec((SUPER_ROWS, PACKED_WIDTH), lambda i, f, e, n: (i, 0)),
            pl.BlockSpec((1, D_MODEL, FFN_TILE), lambda i, f, e, n: (e[i], 0, f)),
            pl.BlockSpec((1, D_MODEL, FFN_TILE), lambda i, f, e, n: (e[i], 0, f + nf)),
            pl.BlockSpec((1, D_FF, D_MODEL), lambda i, f, e, n: (e[i], 0, 0)),
            pl.BlockSpec((1, 1, FFN_TILE), lambda i, f, e, n: (e[i], 0, f)),
            pl.BlockSpec((1, 1, FFN_TILE), lambda i, f, e, n: (e[i], 0, f + nf)),
            pl.BlockSpec((1, 1, D_MODEL), lambda i, f, e, n: (e[i], 0, 0)),
        ],
        out_specs=pl.BlockSpec((SUPER_ROWS, PACKED_WIDTH), lambda i, f, e, n: (i, 0)),
        scratch_shapes=[
            pltpu.VMEM((SUPER_ROWS, D_MODEL), BF16),
            pltpu.VMEM((SUPER_ROWS, D_FF), BF16),
        ],
    )
    return pl.pallas_call(
        _ffn_kernel,
        grid_spec=grid_spec,
        out_shape=jax.ShapeDtypeStruct(xg.shape, jnp.uint32),
        compiler_params=pltpu.CompilerParams(
            dimension_semantics=("arbitrary", "arbitrary"),
            vmem_limit_bytes=VMEM_LIMIT_BYTES),
        name="ffn",
    )(sb_expert, sb_nact, xg, wgu, wgu, wd, bgu3, bgu3, bd3)


def _combine_kernel(dcur_ref, dnxt_ref, wt_ref, h1_ref, fw_ref, eo_hbm, o_ref, buf, sem):
    tm = TOKEN_TILE
    i = pl.program_id(0)
    n = pl.num_programs(0)
    slot = lax.rem(i, 2)

    def copy(dref, s, r, j):
        return pltpu.make_async_copy(
            eo_hbm.at[pl.ds(dref[0, 0, TOP_K * r + j], 1)],
            buf.at[s, j, pl.ds(r, 1)],
            sem.at[s])

    def start_all(dref, s):
        def body(r, carry):
            for j in range(TOP_K):
                copy(dref, s, r, j).start(priority=j % 2)
            return carry
        lax.fori_loop(0, tm, body, 0)

    @pl.when(i == 0)
    def _():
        start_all(dcur_ref, 0)

    @pl.when(i + 1 < n)
    def _():
        start_all(dnxt_ref, 1 - slot)

    def wait_body(r, carry):
        for j in range(TOP_K):
            copy(dcur_ref, slot, r, j).wait()
        return carry
    lax.fori_loop(0, tm, wait_body, 0)

    wt = wt_ref[...]
    half = PACKED_WIDTH
    y_lo = h1_ref[:, 0:half]
    y_hi = h1_ref[:, half:D_MODEL]
    for j in range(TOP_K):
        lo, hi = _unpack_bf16_pairs(buf[slot, j])
        y_lo = y_lo + lo * wt[:, j:j + 1]
        y_hi = y_hi + hi * wt[:, j:j + 1]
    ssq = (jnp.sum(y_lo * y_lo, axis=-1, keepdims=True)
           + jnp.sum(y_hi * y_hi, axis=-1, keepdims=True))
    scale = lax.rsqrt(ssq * (1.0 / D_MODEL) + NORM_EPS)
    o_ref[:, 0:half] = y_lo * scale * fw_ref[:, 0:half]
    o_ref[:, half:D_MODEL] = y_hi * scale * fw_ref[:, half:D_MODEL]


def _combine(dest3, wt, h1, final_w, eo):
    n_tiles = dest3.shape[0]
    tm = TOKEN_TILE
    m = h1.shape[0]
    smem = lambda fn: pl.BlockSpec((1, 1, TOP_K * tm), fn, memory_space=pltpu.SMEM)
    return pl.pallas_call(
        _combine_kernel,
        grid=(n_tiles,),
        in_specs=[
            smem(lambda i: (i, 0, 0)),
            smem(lambda i: (jnp.minimum(i + 1, n_tiles - 1), 0, 0)),
            pl.BlockSpec((tm, LANES), lambda i: (i, 0)),
            pl.BlockSpec((tm, D_MODEL), lambda i: (i, 0)),
            pl.BlockSpec((1, D_MODEL), lambda i: (0, 0)),
            pl.BlockSpec(memory_space=pl.ANY),
        ],
        out_specs=pl.BlockSpec((tm, D_MODEL), lambda i: (i, 0)),
        out_shape=jax.ShapeDtypeStruct((m, D_MODEL), F32),
        scratch_shapes=[
            pltpu.VMEM((2, TOP_K, tm, PACKED_WIDTH), jnp.uint32),
            pltpu.SemaphoreType.DMA((2,)),
        ],
        compiler_params=pltpu.CompilerParams(
            dimension_semantics=("arbitrary",),
            vmem_limit_bytes=VMEM_LIMIT_BYTES),
        name="combine",
    )(dest3, dest3, wt, h1, final_w, eo)


def _pad_lanes(v, width=LANES):
    return jnp.pad(v, [(0, 0)] * (v.ndim - 1) + [(0, width - v.shape[-1])])


def _layer(h, norm_mix_w, in_proj_w, conv_w, conv_b, dt_bias, a_log, d_skip, ssd_norm_w,
           gate_up_w, gate_up_b, gla_norm_w, out_proj_w, norm_ffn_w, router_w, router_b,
           w_gate_up, b_gate_up, w_down, b_down, out_norm_w):
    batch, seq, _ = h.shape
    m = batch * seq
    x2 = h.reshape(m, D_MODEL)
    L = MIX_CHUNK

    w = in_proj_w
    o_xbc = SSD_WIDTH
    o_dt = o_xbc + SSD_CONV_DIM
    o_q = o_dt + SSD_HEADS
    o_k = o_q + GLA_KEY_WIDTH
    o_v = o_k + GLA_KEY_WIDTH
    o_g = o_v + GLA_VALUE_WIDTH
    o_gl = o_g + GLA_VALUE_WIDTH
    w_main = jnp.concatenate([
        w[:, 0:SSD_WIDTH], w[:, o_xbc:o_xbc + SSD_WIDTH], w[:, o_v:o_g], w[:, o_g:o_gl],
        w[:, o_xbc + SSD_WIDTH:o_dt], w[:, o_q:o_k], w[:, o_k:o_v]], axis=1).astype(BF16)
    w_small = _pad_lanes(jnp.concatenate([w[:, o_dt:o_q], w[:, o_gl:o_gl + GLA_GATE_RANK]],
                                         axis=1))
    ws_hi, ws_lo = _split2(w_small)

    u, small = _in_proj(x2, norm_mix_w.reshape(1, D_MODEL), w_main, ws_hi, ws_lo,
                        tm=min(1024, m), tn=1024)

    tri = jnp.tril(jnp.ones((L, L), F32)).astype(BF16)
    head_of_col = jnp.arange(SSD_WIDTH) // SSD_HEAD_DIM
    expand = (jnp.arange(LANES)[:, None] == head_of_col[None, :]).astype(BF16)
    e2 = jnp.concatenate([expand, expand], axis=0)
    y_ssd = _ssd(u, small, conv_w, conv_b.reshape(1, -1),
                 _pad_lanes(dt_bias.reshape(1, -1)), _pad_lanes(a_log.reshape(1, -1)),
                 jnp.repeat(d_skip, SSD_HEAD_DIM).reshape(1, -1), ssd_norm_w.reshape(1, -1),
                 e2, tri, batch=batch, seq=seq)

    wup = jnp.zeros((SMALL_WIDTH, GLA_KEY_WIDTH), F32)
    wup = wup.at[SMALL_GATE_OFF:SMALL_GATE_OFF + GLA_GATE_RANK].set(gate_up_w)
    wup_hi, wup_lo = _split2(wup)
    y_gla = _gla(u, small, wup_hi, wup_lo, gate_up_b.reshape(1, -1), gla_norm_w.reshape(1, -1),
                 tri, batch=batch, seq=seq)

    tm = TOKEN_TILE
    wr_hi, wr_lo = _split2(_pad_lanes(router_w))
    stri = jnp.tril(jnp.ones((tm, tm), F32), k=-1).astype(BF16)
    wo = out_proj_w.astype(BF16)
    h1, n2, top_idx, top_w, rank, cnt = _out_proj(
        y_ssd, y_gla, x2, wo[:SSD_WIDTH], wo[SSD_WIDTH:], norm_ffn_w.reshape(1, -1),
        wr_hi, wr_lo, _pad_lanes(router_b.reshape(1, -1)), stri, tm=tm)

    counts = cnt[0, :N_EXPERTS]
    n_sb = (counts + SUPER_ROWS - 1) // SUPER_ROWS
    sb_end = jnp.cumsum(n_sb)
    sb_start = sb_end - n_sb
    dest = (sb_start * SUPER_ROWS)[top_idx[:, :TOP_K]] + rank[:, :TOP_K]
    dest3 = dest.astype(jnp.int32).reshape(m // tm, 1, TOP_K * tm)

    n_super = (m * TOP_K + N_EXPERTS * (SUPER_ROWS - 1)) // SUPER_ROWS
    sb_ids = jnp.arange(n_super)
    total_sb = sb_end[-1]
    sb_live = jnp.minimum(sb_ids, total_sb - 1)
    sb_expert = jnp.minimum(jnp.searchsorted(sb_end, sb_live, side="right"), N_EXPERTS - 1)
    rows_left = counts[sb_expert] - (sb_live - sb_start[sb_expert]) * SUPER_ROWS
    sb_nact = jnp.where(sb_ids < total_sb,
                        (jnp.clip(rows_left, 0, SUPER_ROWS) + ROW_BLOCK - 1) // ROW_BLOCK, 0)

    sb_rows = jnp.where(sb_ids < total_sb, jnp.clip(rows_left, 0, SUPER_ROWS), 0)
    sub_rows = sb_rows[:, None] - jnp.arange(SUB_PER_SUPER)[None, :] * ROW_BLOCK
    fill = (sub_rows < ROW_BLOCK).astype(jnp.int32).reshape(-1)
    xg = _dispatch(fill, dest3, n2)
    eo = _ffn(sb_expert.astype(jnp.int32), sb_nact.astype(jnp.int32), xg,
              w_gate_up.astype(BF16), w_down.astype(BF16),
              b_gate_up.reshape(N_EXPERTS, 1, 2 * D_FF), b_down.reshape(N_EXPERTS, 1, D_MODEL))
    out = _combine(dest3, top_w, h1, out_norm_w.reshape(1, -1), eo)
    return out.reshape(batch, seq, D_MODEL)


def kernel(x, norm_mix_w, in_proj_w, conv_w, conv_b, dt_bias, a_log, d_skip, ssd_norm_w,
           gate_up_w, gate_up_b, gla_norm_w, out_proj_w, norm_ffn_w, router_w, router_b,
           w_gate_up, b_gate_up, w_down, b_down, final_norm_w):
    assert norm_mix_w.shape[0] == 1
    return _layer(x, norm_mix_w[0], in_proj_w[0], conv_w[0], conv_b[0], dt_bias[0], a_log[0],
                  d_skip[0], ssd_norm_w[0], gate_up_w[0], gate_up_b[0], gla_norm_w[0],
                  out_proj_w[0], norm_ffn_w[0], router_w[0], router_b[0], w_gate_up[0],
                  b_gate_up[0], w_down[0], b_down[0], final_norm_w)
```

```python
import functools

import jax
import jax.numpy as jnp
from jax import lax
from jax.experimental import pallas as pl
from jax.experimental.pallas import tpu as pltpu

F32 = jnp.float32
BF16 = jnp.bfloat16

D_MODEL = 2048
NORM_EPS = 1e-5

SSD_WIDTH = 2048
SSD_HEAD_DIM = 64
SSD_HEADS = 32
SSD_GROUPS = 4
SSD_STATE = 128
CONV_WIDTH = 4
SSD_GROUP_WIDTH = SSD_WIDTH // SSD_GROUPS
SSD_BC_WIDTH = 2 * SSD_GROUPS * SSD_STATE
SSD_CONV_DIM = SSD_WIDTH + SSD_BC_WIDTH

GLA_HEADS = 4
GLA_KEY_WIDTH = 1024
GLA_VALUE_WIDTH = 2048
GLA_HEAD_K = 256
GLA_HEAD_V = 512
GLA_GATE_RANK = 16
GLA_GATE_NORMALIZER = 16.0

N_EXPERTS = 32
TOP_K = 4
D_FF = 2048
SWIGLU_LIMIT = 7.0
SWIGLU_ALPHA = 1.702

LANES = 128
SUBLANES = 8
VMEM_LIMIT_BYTES = 56 * 1024 * 1024

U_WIDTH = 11264
SMALL_WIDTH = LANES
SMALL_DT_OFF = 0
SMALL_GATE_OFF = SSD_HEADS

MIX_CHUNK = 128
GLA_SUB = 16
GLA_EXP_CLAMP = 80.0

ROW_BLOCK = 256
SUPER_ROWS = 1024
SUB_PER_SUPER = SUPER_ROWS // ROW_BLOCK
PACKED_WIDTH = D_MODEL // 2
FFN_TILE = 512
TOKEN_TILE = 256


def _split2(v):
    hi = v.astype(BF16)
    lo = (v - hi.astype(F32)).astype(BF16)
    return hi, lo


def _split3(v):
    hi = v.astype(BF16)
    r = v - hi.astype(F32)
    mid = r.astype(BF16)
    lo = (r - mid.astype(F32)).astype(BF16)
    return hi, mid, lo


def _dot(a, b):
    return jnp.dot(a, b, preferred_element_type=F32)


def _dot_nt(a, b):
    return lax.dot_general(a, b, (((1,), (1,)), ((), ())), preferred_element_type=F32)


def _dot_exact_lhs(mat01, v):
    hi, mid, lo = _split3(v)
    return _dot(mat01, hi) + _dot(mat01, mid) + _dot(mat01, lo)


def _dot_hilo(a, b_hi, b_lo):
    a_hi, a_lo = _split2(a)
    return _dot(a_hi, b_hi) + _dot(a_lo, b_hi) + _dot(a_hi, b_lo)


def _pack_bf16_pairs(v):
    c = v.shape[1] // 2
    lo = lax.bitcast_convert_type(v[:, :c].astype(BF16).astype(F32), jnp.uint32)
    hi = lax.bitcast_convert_type(v[:, c:].astype(BF16).astype(F32), jnp.uint32)
    return (lo >> 16) | (hi & jnp.uint32(0xFFFF0000))


def _unpack_bf16_pairs(w):
    lo = lax.bitcast_convert_type(w << 16, F32)
    hi = lax.bitcast_convert_type(w & jnp.uint32(0xFFFF0000), F32)
    return lo, hi


def _silu(v):
    return v * jax.nn.sigmoid(v)


def _softplus(v):
    return jnp.maximum(v, 0.0) + jnp.log1p(jnp.exp(-jnp.abs(v)))


def _log_sigmoid(v):
    return jnp.minimum(v, 0.0) - jnp.log1p(jnp.exp(-jnp.abs(v)))


def _in_proj_kernel(x_ref, nw_ref, wt_ref, wsh_ref, wsl_ref, u_ref, small_ref, n1_ref):
    @pl.when(pl.program_id(1) == 0)
    def _():
        x = x_ref[...]
        ms = jnp.mean(x * x, axis=-1, keepdims=True)
        n1 = x * lax.rsqrt(ms + NORM_EPS) * nw_ref[...]
        n1_ref[...] = n1.astype(BF16)
        n1_hi, n1_lo = _split2(n1)
        small_ref[...] = (_dot_nt(n1_hi, wsh_ref[...]) + _dot_nt(n1_lo, wsh_ref[...])
                          + _dot_nt(n1_hi, wsl_ref[...]))

    u_ref[...] = _dot_nt(n1_ref[...], wt_ref[...]).astype(BF16)


def _in_proj(x2, norm_w, wt_main, ws_hi, ws_lo, *, tm, tn):
    m = x2.shape[0]
    grid = (m // tm, U_WIDTH // tn)
    return pl.pallas_call(
        _in_proj_kernel,
        grid=grid,
        in_specs=[
            pl.BlockSpec((tm, D_MODEL), lambda i, j: (i, 0)),
            pl.BlockSpec((1, D_MODEL), lambda i, j: (0, 0)),
            pl.BlockSpec((tn, D_MODEL), lambda i, j: (j, 0)),
            pl.BlockSpec((SMALL_WIDTH, D_MODEL), lambda i, j: (0, 0)),
            pl.BlockSpec((SMALL_WIDTH, D_MODEL), lambda i, j: (0, 0)),
        ],
        out_specs=[
            pl.BlockSpec((tm, tn), lambda i, j: (i, j)),
            pl.BlockSpec((tm, SMALL_WIDTH), lambda i, j: (i, 0)),
        ],
        out_shape=[
            jax.ShapeDtypeStruct((m, U_WIDTH), BF16),
            jax.ShapeDtypeStruct((m, SMALL_WIDTH), F32),
        ],
        scratch_shapes=[pltpu.VMEM((tm, D_MODEL), BF16)],
        compiler_params=pltpu.CompilerParams(
            dimension_semantics=("arbitrary", "arbitrary"),
            vmem_limit_bytes=VMEM_LIMIT_BYTES),
        name="in_proj",
    )(x2, norm_w, wt_main, ws_hi, ws_lo)


def _ssd_kernel(z_ref, x_ref, bc_ref, sm_ref, convw_ref, convb_ref, dtb_ref, alog_ref,
                dsk_ref, nw_ref, e2_ref, tri_ref, o_ref,
                ext_ref, xc_ref, xd_ref, xdd_ref, y_ref, state_ref):
    L = MIX_CHUNK
    G, N, P = SSD_GROUPS, SSD_STATE, SSD_HEAD_DIM
    GW = SSD_GROUP_WIDTH

    @pl.when(pl.program_id(1) == 0)
    def _():
        ext_ref[0:SUBLANES, :] = jnp.zeros((SUBLANES, SSD_CONV_DIM), F32)
        state_ref[...] = jnp.zeros(state_ref.shape, F32)

    ext_ref[SUBLANES:SUBLANES + L, 0:SSD_WIDTH] = x_ref[...].astype(F32)
    ext_ref[SUBLANES:SUBLANES + L, SSD_WIDTH:SSD_CONV_DIM] = bc_ref[...].astype(F32)
    slab = 512
    for s in range(SSD_CONV_DIM // slab):
        cols = slice(s * slab, (s + 1) * slab)
        acc = jnp.broadcast_to(convb_ref[:, cols], (L, slab))
        for w in range(CONV_WIDTH):
            off = SUBLANES - (CONV_WIDTH - 1) + w
            acc = acc + convw_ref[w:w + 1, cols] * ext_ref[off:off + L, cols]
        xc_ref[:, cols] = _silu(acc)
    ext_ref[0:SUBLANES, :] = ext_ref[L:L + SUBLANES, :]

    lane = lax.broadcasted_iota(jnp.int32, (1, LANES), 1)
    a = jnp.where(lane < SSD_HEADS, -jnp.exp(alog_ref[...]), 0.0)
    dt = _softplus(sm_ref[...] + dtb_ref[...])
    la = _dot_exact_lhs(tri_ref[...], dt * a)
    la_t = la.T
    la_end = la[L - 1:L, :]

    def expand(v):
        hi, lo = _split2(v)
        return _dot(jnp.concatenate([hi, lo], axis=1), e2_ref[...])

    dt_e = expand(dt)
    ela_e = expand(jnp.exp(la))
    dte_e = expand(jnp.exp(la_end - la))
    cd_e = expand(jnp.broadcast_to(jnp.exp(la_end), (SUBLANES, LANES)))[0:1, :]

    for s in range(SSD_WIDTH // slab):
        cols = slice(s * slab, (s + 1) * slab)
        xd = xc_ref[:, cols] * dt_e[:, cols]
        xd_ref[:, cols] = xd
        xdd_ref[:, cols] = (xd * dte_e[:, cols]).astype(BF16)

    row = lax.broadcasted_iota(jnp.int32, (L, L), 0)
    col = lax.broadcasted_iota(jnp.int32, (L, L), 1)
    causal = row >= col
    lane_l = lax.broadcasted_iota(jnp.int32, (L, LANES), 1)

    for g in range(G):
        bm = xc_ref[:, SSD_WIDTH + g * N:SSD_WIDTH + (g + 1) * N]
        cm = xc_ref[:, SSD_WIDTH + G * N + g * N:SSD_WIDTH + G * N + (g + 1) * N]
        cm_b = cm.astype(BF16)
        cb = _dot_nt(cm_b, bm.astype(BF16))
        gcols = slice(g * GW, (g + 1) * GW)
        y_off = _dot(cm_b, state_ref[g].astype(BF16)) * ela_e[:, gcols]
        for p in range(GW // LANES):
            h0 = g * (GW // P) + 2 * p
            ms = []
            for h in (h0, h0 + 1):
                seg = la[:, h:h + 1] - la_t[h:h + 1, :]
                dec = jnp.where(causal, jnp.exp(jnp.minimum(seg, 0.0)), 0.0)
                ms.append((cb * dec).astype(BF16))
            pcols = slice(g * GW + p * LANES, g * GW + (p + 1) * LANES)
            xp = xd_ref[:, pcols]
            rhs = jnp.concatenate([jnp.where(lane_l < P, xp, 0.0),
                                   jnp.where(lane_l >= P, xp, 0.0)], axis=0).astype(BF16)
            y_pair = _dot(jnp.concatenate(ms, axis=1), rhs)
            y_ref[:, pcols] = (y_pair + y_off[:, p * LANES:(p + 1) * LANES]
                               + dsk_ref[:, pcols] * xc_ref[:, pcols])
        upd = _dot(bm.T.astype(BF16), xdd_ref[:, gcols])
        state_ref[g] = cd_e[:, gcols] * state_ref[g] + upd

    for g in range(G):
        gcols = slice(g * GW, (g + 1) * GW)
        y = y_ref[:, gcols] * _silu(z_ref[:, gcols].astype(F32))
        ms = jnp.mean(y * y, axis=-1, keepdims=True)
        o_ref[:, gcols] = (y * lax.rsqrt(ms + NORM_EPS) * nw_ref[:, gcols]).astype(BF16)


def _ssd(u, small, conv_w, conv_b, dtb, alog, dsk_e, norm_w, e2, tri, *, batch, seq):
    L = MIX_CHUNK
    nc = seq // L
    m = batch * seq

    def rows(b, c):
        return b * nc + c

    const = lambda shape: pl.BlockSpec(shape, lambda b, c: (0,) * len(shape))
    return pl.pallas_call(
        _ssd_kernel,
        grid=(batch, nc),
        in_specs=[
            pl.BlockSpec((L, SSD_WIDTH), lambda b, c: (rows(b, c), 0)),
            pl.BlockSpec((L, SSD_WIDTH), lambda b, c: (rows(b, c), 1)),
            pl.BlockSpec((L, SSD_BC_WIDTH), lambda b, c: (rows(b, c), 8)),
            pl.BlockSpec((L, SMALL_WIDTH), lambda b, c: (rows(b, c), 0)),
            const((CONV_WIDTH, SSD_CONV_DIM)),
            const((1, SSD_CONV_DIM)),
            const((1, LANES)),
            const((1, LANES)),
            const((1, SSD_WIDTH)),
            const((1, SSD_WIDTH)),
            const((2 * LANES, SSD_WIDTH)),
            const((L, L)),
        ],
        out_specs=pl.BlockSpec((L, SSD_WIDTH), lambda b, c: (rows(b, c), 0)),
        out_shape=jax.ShapeDtypeStruct((m, SSD_WIDTH), BF16),
        scratch_shapes=[
            pltpu.VMEM((L + 2 * SUBLANES, SSD_CONV_DIM), F32),
            pltpu.VMEM((L, SSD_CONV_DIM), F32),
            pltpu.VMEM((L, SSD_WIDTH), F32),
            pltpu.VMEM((L, SSD_WIDTH), BF16),
            pltpu.VMEM((L, SSD_WIDTH), F32),
            pltpu.VMEM((SSD_GROUPS, SSD_STATE, SSD_GROUP_WIDTH), F32),
        ],
        compiler_params=pltpu.CompilerParams(
            dimension_semantics=("arbitrary", "arbitrary"),
            vmem_limit_bytes=VMEM_LIMIT_BYTES),
        name="ssd",
    )(u, u, u, small, conv_w, conv_b, dtb, alog, dsk_e, norm_w, e2, tri)


def _gla_kernel(q_ref, k_ref, v_ref, g_ref, sm_ref, wuh_ref, wul_ref, bup_ref, nw_ref,
                tri_ref, o_ref, cum_ref, s_ref, st_ref):
    L, C = MIX_CHUNK, GLA_SUB
    K, V = GLA_HEAD_K, GLA_HEAD_V

    @pl.when(pl.program_id(1) == 0)
    def _():
        st_ref[...] = jnp.zeros(st_ref.shape, F32)

    pre = _dot_hilo(sm_ref[...], wuh_ref[...], wul_ref[...]) + bup_ref[...]
    gk = _log_sigmoid(pre) * (1.0 / GLA_GATE_NORMALIZER)
    cum_ref[...] = _dot_exact_lhs(tri_ref[...], gk)

    row_c = lax.broadcasted_iota(jnp.int32, (C, L), 0)
    col_c = lax.broadcasted_iota(jnp.int32, (C, L), 1)
    for h in range(GLA_HEADS):
        ks = slice(h * K, (h + 1) * K)
        vs = slice(h * V, (h + 1) * V)
        cum = cum_ref[:, ks]
        cum_last = cum_ref[L - 1:L, ks]
        qh = q_ref[:, ks].astype(F32) * (K ** -0.5)
        kh = k_ref[:, ks].astype(F32)
        vh = v_ref[:, vs]
        for blk in range(L // C):
            rows = slice(blk * C, (blk + 1) * C)
            if blk == 0:
                ref_row = jnp.zeros((1, K), F32)
            else:
                ref_row = cum_ref[blk * C - 1:blk * C, ks]
            q_b = (qh[rows] * jnp.exp(cum_ref[rows, ks] - ref_row)).astype(BF16)
            k_b = (kh * jnp.exp(jnp.minimum(ref_row - cum, GLA_EXP_CLAMP))).astype(BF16)
            s_b = _dot_nt(q_b, k_b)
            s_ref[rows, :] = jnp.where(col_c <= row_c + blk * C, s_b, 0.0)
        o_intra = _dot(s_ref[...].astype(BF16), vh)
        st = st_ref[h]
        o_inter = _dot_nt((qh * jnp.exp(cum)).astype(BF16), st.astype(BF16))
        k_dec = (kh * jnp.exp(cum_last - cum)).astype(BF16)
        st_ref[h] = st * jnp.exp(cum_last) + _dot(vh.astype(F32).T.astype(BF16), k_dec)
        o = o_inter + o_intra
        ms = jnp.mean(o * o, axis=-1, keepdims=True)
        o = o * lax.rsqrt(ms + NORM_EPS) * nw_ref[...] * _silu(g_ref[:, vs].astype(F32))
        o_ref[:, vs] = o.astype(BF16)


def _gla(u, small, wup_hi, wup_lo, bup, norm_w, tri, *, batch, seq):
    L = MIX_CHUNK
    nc = seq // L
    m = batch * seq

    def rows(b, c):
        return b * nc + c

    const = lambda shape: pl.BlockSpec(shape, lambda b, c: (0,) * len(shape))
    return pl.pallas_call(
        _gla_kernel,
        grid=(batch, nc),
        in_specs=[
            pl.BlockSpec((L, GLA_KEY_WIDTH), lambda b, c: (rows(b, c), 9)),
            pl.BlockSpec((L, GLA_KEY_WIDTH), lambda b, c: (rows(b, c), 10)),
            pl.BlockSpec((L, GLA_VALUE_WIDTH), lambda b, c: (rows(b, c), 2)),
            pl.BlockSpec((L, GLA_VALUE_WIDTH), lambda b, c: (rows(b, c), 3)),
            pl.BlockSpec((L, SMALL_WIDTH), lambda b, c: (rows(b, c), 0)),
            const((SMALL_WIDTH, GLA_KEY_WIDTH)),
            const((SMALL_WIDTH, GLA_KEY_WIDTH)),
            const((1, GLA_KEY_WIDTH)),
            const((1, GLA_HEAD_V)),
            const((L, L)),
        ],
        out_specs=pl.BlockSpec((L, GLA_VALUE_WIDTH), lambda b, c: (rows(b, c), 0)),
        out_shape=jax.ShapeDtypeStruct((m, GLA_VALUE_WIDTH), BF16),
        scratch_shapes=[
            pltpu.VMEM((L, GLA_KEY_WIDTH), F32),
            pltpu.VMEM((L, L), F32),
            pltpu.VMEM((GLA_HEADS, GLA_HEAD_V, GLA_HEAD_K), F32),
        ],
        compiler_params=pltpu.CompilerParams(
            dimension_semantics=("arbitrary", "arbitrary"),
            vmem_limit_bytes=VMEM_LIMIT_BYTES),
        name="gla",
    )(u, u, u, u, small, wup_hi, wup_lo, bup, norm_w, tri)


def _out_proj_kernel(ys_ref, yg_ref, x_ref, wo1_ref, wo2_ref, nw_ref, wrh_ref, wrl_ref,
                     rb_ref, stri_ref, h1_ref, n2_ref, idx_ref, wt_ref, rank_ref, cnt_ref,
                     run_ref):
    tm = x_ref.shape[0]

    @pl.when(pl.program_id(0) == 0)
    def _():
        run_ref[...] = jnp.zeros(run_ref.shape, F32)

    h = x_ref[...] + _dot(ys_ref[...], wo1_ref[...]) + _dot(yg_ref[...], wo2_ref[...])
    h1_ref[...] = h
    ms = jnp.mean(h * h, axis=-1, keepdims=True)
    n2 = h * lax.rsqrt(ms + NORM_EPS) * nw_ref[...]
    n2_ref[...] = _pack_bf16_pairs(n2)

    lane = lax.broadcasted_iota(jnp.int32, (tm, LANES), 1)
    lane_f = lane.astype(F32)
    logits = _dot_hilo(n2, wrh_ref[...], wrl_ref[...]) + rb_ref[...]
    work = jnp.where(lane < N_EXPERTS, logits, -jnp.inf)
    vals, idxs, sels = [], [], []
    for _ in range(TOP_K):
        top = jnp.max(work, axis=-1, keepdims=True)
        idx = jnp.min(jnp.where(work == top, lane_f, float(LANES)), axis=-1, keepdims=True)
        sel = lane_f == idx
        vals.append(top)
        idxs.append(idx)
        sels.append(sel)
        work = jnp.where(sel, -jnp.inf, work)
    exps = [jnp.exp(v - vals[0]) for v in vals]
    denom = exps[0] + exps[1] + exps[2] + exps[3]

    chosen = jnp.zeros((tm, LANES), F32)
    for sel in sels:
        chosen = chosen + jnp.where(sel, 1.0, 0.0)
    before = run_ref[0:1, :] + _dot(stri_ref[...], chosen.astype(BF16))

    idx_out = jnp.zeros((tm, LANES), F32)
    wt_out = jnp.zeros((tm, LANES), F32)
    rank_out = jnp.zeros((tm, LANES), F32)
    for j in range(TOP_K):
        rank_j = jnp.sum(jnp.where(sels[j], before, 0.0), axis=-1, keepdims=True)
        idx_out = jnp.where(lane == j, idxs[j], idx_out)
        wt_out = jnp.where(lane == j, exps[j] / denom, wt_out)
        rank_out = jnp.where(lane == j, rank_j, rank_out)
    idx_ref[...] = idx_out.astype(jnp.int32)
    wt_ref[...] = wt_out
    rank_ref[...] = rank_out.astype(jnp.int32)

    run_ref[...] = run_ref[...] + jnp.sum(chosen, axis=0, keepdims=True)
    cnt_ref[...] = run_ref[...].astype(jnp.int32)


def _out_proj(y_ssd, y_gla, x2, wo1, wo2, norm_w, wr_hi, wr_lo, rb, stri, *, tm):
    m = x2.shape[0]
    const = lambda shape: pl.BlockSpec(shape, lambda i: (0,) * len(shape))
    tile = lambda w: pl.BlockSpec((tm, w), lambda i: (i, 0))
    return pl.pallas_call(
        _out_proj_kernel,
        grid=(m // tm,),
        in_specs=[
            tile(SSD_WIDTH), tile(GLA_VALUE_WIDTH), tile(D_MODEL),
            const((SSD_WIDTH, D_MODEL)), const((GLA_VALUE_WIDTH, D_MODEL)),
            const((1, D_MODEL)),
            const((D_MODEL, LANES)), const((D_MODEL, LANES)), const((1, LANES)),
            const((tm, tm)),
        ],
        out_specs=[
            tile(D_MODEL), tile(PACKED_WIDTH), tile(LANES), tile(LANES), tile(LANES),
            const((SUBLANES, LANES)),
        ],
        out_shape=[
            jax.ShapeDtypeStruct((m, D_MODEL), F32),
            jax.ShapeDtypeStruct((m, PACKED_WIDTH), jnp.uint32),
            jax.ShapeDtypeStruct((m, LANES), jnp.int32),
            jax.ShapeDtypeStruct((m, LANES), F32),
            jax.ShapeDtypeStruct((m, LANES), jnp.int32),
            jax.ShapeDtypeStruct((SUBLANES, LANES), jnp.int32),
        ],
        scratch_shapes=[pltpu.VMEM((SUBLANES, LANES), F32)],
        compiler_params=pltpu.CompilerParams(
            dimension_semantics=("arbitrary",),
            vmem_limit_bytes=VMEM_LIMIT_BYTES),
        name="out_proj",
    )(y_ssd, y_gla, x2, wo1, wo2, norm_w, wr_hi, wr_lo, rb, stri)


def _dispatch_kernel(fill_ref, dest_ref, n2_ref, xg_hbm, zero_ref, sem, zsem):
    tm = TOKEN_TILE

    @pl.when(pl.program_id(0) == 0)
    def _():
        zero_ref[...] = jnp.zeros(zero_ref.shape, zero_ref.dtype)

        def fill(b):
            row = pl.multiple_of(b * ROW_BLOCK, ROW_BLOCK)
            return pltpu.make_async_copy(zero_ref, xg_hbm.at[pl.ds(row, ROW_BLOCK)], zsem)

        def start_fill(b, carry):
            @pl.when(fill_ref[b] > 0)
            def _():
                fill(b).start()
            return carry

        def wait_fill(b, carry):
            @pl.when(fill_ref[b] > 0)
            def _():
                fill(b).wait()
            return carry

        lax.fori_loop(0, fill_ref.shape[0], start_fill, 0)
        lax.fori_loop(0, fill_ref.shape[0], wait_fill, 0)

    def copy(r, row):
        return pltpu.make_async_copy(n2_ref.at[pl.ds(r, 1)], xg_hbm.at[pl.ds(row, 1)], sem)

    for r in range(tm):
        for j in range(TOP_K):
            copy(r, dest_ref[0, 0, TOP_K * r + j]).start(priority=j % 2)
    for r in range(tm):
        for j in range(TOP_K):
            copy(r, 0).wait()


def _dispatch(fill, dest3, n2p):
    n_tiles = dest3.shape[0]
    n_rows = fill.shape[0] * ROW_BLOCK
    grid_spec = pltpu.PrefetchScalarGridSpec(
        num_scalar_prefetch=1,
        grid=(n_tiles,),
        in_specs=[
            pl.BlockSpec((1, 1, TOP_K * TOKEN_TILE), lambda i, fl: (i, 0, 0),
                         memory_space=pltpu.SMEM),
            pl.BlockSpec((TOKEN_TILE, PACKED_WIDTH), lambda i, fl: (i, 0)),
        ],
        out_specs=pl.BlockSpec(memory_space=pl.ANY),
        scratch_shapes=[
            pltpu.VMEM((ROW_BLOCK, PACKED_WIDTH), jnp.uint32),
            pltpu.SemaphoreType.DMA(()),
            pltpu.SemaphoreType.DMA(()),
        ],
    )
    return pl.pallas_call(
        _dispatch_kernel,
        grid_spec=grid_spec,
        out_shape=jax.ShapeDtypeStruct((n_rows, PACKED_WIDTH), jnp.uint32),
        compiler_params=pltpu.CompilerParams(
            dimension_semantics=("arbitrary",), has_side_effects=True),
        name="dispatch",
    )(fill, dest3, n2p)


def _ffn_kernel(sbe_ref, nact_ref, x_ref, wg_ref, wu_ref, wd_ref, bg_ref, bu_ref, bd_ref,
                o_ref, xb_ref, act_ref):
    del sbe_ref
    i = pl.program_id(0)
    f = pl.program_id(1)
    nf = D_FF // FFN_TILE
    nact = nact_ref[i]
    for sb in range(SUB_PER_SUPER):
        rows = slice(sb * ROW_BLOCK, (sb + 1) * ROW_BLOCK)

        @pl.when(sb < nact)
        def _():
            @pl.when(f == 0)
            def _():
                lo, hi = _unpack_bf16_pairs(x_ref[rows, :])
                xb_ref[rows, 0:PACKED_WIDTH] = lo.astype(BF16)
                xb_ref[rows, PACKED_WIDTH:D_MODEL] = hi.astype(BF16)

            xs = xb_ref[rows, :]
            gate = jnp.minimum(_dot(xs, wg_ref[0]) + bg_ref[0], SWIGLU_LIMIT)
            up = jnp.clip(_dot(xs, wu_ref[0]) + bu_ref[0], -SWIGLU_LIMIT, SWIGLU_LIMIT)
            act = ((up + 1.0) * gate * jax.nn.sigmoid(SWIGLU_ALPHA * gate)).astype(BF16)
            for k in range(nf):
                @pl.when(f == k)
                def _():
                    act_ref[rows, k * FFN_TILE:(k + 1) * FFN_TILE] = act

            @pl.when(f == nf - 1)
            def _():
                out = _dot(act_ref[rows, :], wd_ref[0]) + bd_ref[0]
                o_ref[rows, :] = _pack_bf16_pairs(out)

        @pl.when(jnp.logical_and(sb >= nact, f == nf - 1))
        def _():
            o_ref[rows, :] = jnp.zeros((ROW_BLOCK, PACKED_WIDTH), jnp.uint32)


def _ffn(sb_expert, sb_nact, xg, wgu, wd, bgu3, bd3):
    n_super = sb_expert.shape[0]
    nf = D_FF // FFN_TILE
    grid_spec = pltpu.PrefetchScalarGridSpec(
        num_scalar_prefetch=2,
        grid=(n_super, nf),
        in_specs=[
            pl.BlockSpec((SUPER_ROWS, PACKED_WIDTH), lambda i, f, e, n: (i, 0)),
            pl.BlockSpec((1, D_MODEL, FFN_TILE), lambda i, f, e, n: (e[i], 0, f)),
            pl.BlockSpec((1, D_MODEL, FFN_TILE), lambda i, f, e, n: (e[i], 0, f + nf)),
            pl.BlockSpec((1, D_FF, D_MODEL), lambda i, f, e, n: (e[i], 0, 0)),
            pl.BlockSpec((1, 1, FFN_TILE), lambda i, f, e, n: (e[i], 0, f)),
            pl.BlockSpec((1, 1, FFN_TILE), lambda i, f, e, n: (e[i], 0, f + nf)),
            pl.BlockSpec((1, 1, D_MODEL), lambda i, f, e, n: (e[i], 0, 0)),
        ],
        out_specs=pl.BlockSpec((SUPER_ROWS, PACKED_WIDTH), lambda i, f, e, n: (i, 0)),
        scratch_shapes=[
            pltpu.VMEM((SUPER_ROWS, D_MODEL), BF16),
            pltpu.VMEM((SUPER_ROWS, D_FF), BF16),
        ],
    )
    return pl.pallas_call(
        _ffn_kernel,
        grid_spec=grid_spec,
        out_shape=jax.ShapeDtypeStruct(xg.shape, jnp.uint32),
        compiler_params=pltpu.CompilerParams(
            dimension_semantics=("arbitrary", "arbitrary"),
            vmem_limit_bytes=VMEM_LIMIT_BYTES),
        name="ffn",
    )(sb_expert, sb_nact, xg, wgu, wgu, wd, bgu3, bgu3, bd3)


def _combine_kernel(dcur_ref, dnxt_ref, wt_ref, h1_ref, fw_ref, eo_hbm, o_ref, buf, sem):
    tm = TOKEN_TILE
    i = pl.program_id(0)
    n = pl.num_programs(0)
    slot = lax.rem(i, 2)

    def copy(row, s, r, j):
        return pltpu.make_async_copy(
            eo_hbm.at[pl.ds(row, 1)], buf.at[s, j, pl.ds(r, 1)], sem.at[s])

    def start_all(dref, s):
        for r in range(tm):
            for j in range(TOP_K):
                copy(dref[0, 0, TOP_K * r + j], s, r, j).start(priority=j % 2)

    def wait_all(s):
        for r in range(tm):
            for j in range(TOP_K):
                copy(0, s, r, j).wait()

    @pl.when(i == 0)
    def _():
        start_all(dcur_ref, 0)

    for s in range(2):
        @pl.when(jnp.logical_and(i + 1 < n, slot != s))
        def _():
            start_all(dnxt_ref, s)

    for s in range(2):
        @pl.when(slot == s)
        def _():
            wait_all(s)

    wt = wt_ref[...]
    half = PACKED_WIDTH
    y_lo = h1_ref[:, 0:half]
    y_hi = h1_ref[:, half:D_MODEL]
    for j in range(TOP_K):
        lo, hi = _unpack_bf16_pairs(buf[slot, j])
        y_lo = y_lo + lo * wt[:, j:j + 1]
        y_hi = y_hi + hi * wt[:, j:j + 1]
    ssq = (jnp.sum(y_lo * y_lo, axis=-1, keepdims=True)
           + jnp.sum(y_hi * y_hi, axis=-1, keepdims=True))
    scale = lax.rsqrt(ssq * (1.0 / D_MODEL) + NORM_EPS)
    o_ref[:, 0:half] = y_lo * scale * fw_ref[:, 0:half]
    o_ref[:, half:D_MODEL] = y_hi * scale * fw_ref[:, half:D_MODEL]


def _combine(dest3, wt, h1, final_w, eo):
    n_tiles = dest3.shape[0]
    tm = TOKEN_TILE
    m = h1.shape[0]
    smem = lambda fn: pl.BlockSpec((1, 1, TOP_K * tm), fn, memory_space=pltpu.SMEM)
    return pl.pallas_call(
        _combine_kernel,
        grid=(n_tiles,),
        in_specs=[
            smem(lambda i: (i, 0, 0)),
            smem(lambda i: (jnp.minimum(i + 1, n_tiles - 1), 0, 0)),
            pl.BlockSpec((tm, LANES), lambda i: (i, 0)),
            pl.BlockSpec((tm, D_MODEL), lambda i: (i, 0)),
            pl.BlockSpec((1, D_MODEL), lambda i: (0, 0)),
            pl.BlockSpec(memory_space=pl.ANY),
        ],
        out_specs=pl.BlockSpec((tm, D_MODEL), lambda i: (i, 0)),
        out_shape=jax.ShapeDtypeStruct((m, D_MODEL), F32),
        scratch_shapes=[
            pltpu.VMEM((2, TOP_K, tm, PACKED_WIDTH), jnp.uint32),
            pltpu.SemaphoreType.DMA((2,)),
        ],
        compiler_params=pltpu.CompilerParams(
            dimension_semantics=("arbitrary",),
            vmem_limit_bytes=VMEM_LIMIT_BYTES),
        name="combine",
    )(dest3, dest3, wt, h1, final_w, eo)


def _pad_lanes(v, width=LANES):
    return jnp.pad(v, [(0, 0)] * (v.ndim - 1) + [(0, width - v.shape[-1])])


def _layer(h, norm_mix_w, in_proj_w, conv_w, conv_b, dt_bias, a_log, d_skip, ssd_norm_w,
           gate_up_w, gate_up_b, gla_norm_w, out_proj_w, norm_ffn_w, router_w, router_b,
           w_gate_up, b_gate_up, w_down, b_down, out_norm_w):
    batch, seq, _ = h.shape
    m = batch * seq
    x2 = h.reshape(m, D_MODEL)
    L = MIX_CHUNK

    w = in_proj_w.T
    o_xbc = SSD_WIDTH
    o_dt = o_xbc + SSD_CONV_DIM
    o_q = o_dt + SSD_HEADS
    o_k = o_q + GLA_KEY_WIDTH
    o_v = o_k + GLA_KEY_WIDTH
    o_g = o_v + GLA_VALUE_WIDTH
    o_gl = o_g + GLA_VALUE_WIDTH
    wt_main = jnp.concatenate([
        w[0:SSD_WIDTH], w[o_xbc:o_xbc + SSD_WIDTH], w[o_v:o_g], w[o_g:o_gl],
        w[o_xbc + SSD_WIDTH:o_dt], w[o_q:o_k], w[o_k:o_v]], axis=0).astype(BF16)
    n_small = SSD_HEADS + GLA_GATE_RANK
    wt_small = jnp.concatenate([w[o_dt:o_q], w[o_gl:o_gl + GLA_GATE_RANK],
                                jnp.zeros((SMALL_WIDTH - n_small, D_MODEL), F32)], axis=0)
    ws_hi, ws_lo = _split2(wt_small)

    u, small = _in_proj(x2, norm_mix_w.reshape(1, D_MODEL), wt_main, ws_hi, ws_lo,
                        tm=min(1024, m), tn=1024)

    tri = jnp.tril(jnp.ones((L, L), F32)).astype(BF16)
    head_of_col = jnp.arange(SSD_WIDTH) // SSD_HEAD_DIM
    expand = (jnp.arange(LANES)[:, None] == head_of_col[None, :]).astype(BF16)
    e2 = jnp.concatenate([expand, expand], axis=0)
    y_ssd = _ssd(u, small, conv_w, conv_b.reshape(1, -1),
                 _pad_lanes(dt_bias.reshape(1, -1)), _pad_lanes(a_log.reshape(1, -1)),
                 jnp.repeat(d_skip, SSD_HEAD_DIM).reshape(1, -1), ssd_norm_w.reshape(1, -1),
                 e2, tri, batch=batch, seq=seq)

    wup = jnp.zeros((SMALL_WIDTH, GLA_KEY_WIDTH), F32)
    wup = wup.at[SMALL_GATE_OFF:SMALL_GATE_OFF + GLA_GATE_RANK].set(gate_up_w)
    wup_hi, wup_lo = _split2(wup)
    y_gla = _gla(u, small, wup_hi, wup_lo, gate_up_b.reshape(1, -1), gla_norm_w.reshape(1, -1),
                 tri, batch=batch, seq=seq)

    tm = TOKEN_TILE
    wr_hi, wr_lo = _split2(_pad_lanes(router_w))
    stri = jnp.tril(jnp.ones((tm, tm), F32), k=-1).astype(BF16)
    wo = out_proj_w.astype(BF16)
    h1, n2, top_idx, top_w, rank, cnt = _out_proj(
        y_ssd, y_gla, x2, wo[:SSD_WIDTH], wo[SSD_WIDTH:], norm_ffn_w.reshape(1, -1),
        wr_hi, wr_lo, _pad_lanes(router_b.reshape(1, -1)), stri, tm=tm)

    counts = cnt[0, :N_EXPERTS]
    n_sb = (counts + SUPER_ROWS - 1) // SUPER_ROWS
    sb_end = jnp.cumsum(n_sb)
    sb_start = sb_end - n_sb
    dest = (sb_start * SUPER_ROWS)[top_idx[:, :TOP_K]] + rank[:, :TOP_K]
    dest3 = dest.astype(jnp.int32).reshape(m // tm, 1, TOP_K * tm)

    n_super = (m * TOP_K + N_EXPERTS * (SUPER_ROWS - 1)) // SUPER_ROWS
    sb_ids = jnp.arange(n_super)
    total_sb = sb_end[-1]
    sb_live = jnp.minimum(sb_ids, total_sb - 1)
    sb_expert = jnp.minimum(jnp.searchsorted(sb_end, sb_live, side="right"), N_EXPERTS - 1)
    rows_left = counts[sb_expert] - (sb_live - sb_start[sb_expert]) * SUPER_ROWS
    sb_nact = jnp.where(sb_ids < total_sb,
                        (jnp.clip(rows_left, 0, SUPER_ROWS) + ROW_BLOCK - 1) // ROW_BLOCK, 0)

    sb_rows = jnp.where(sb_ids < total_sb, jnp.clip(rows_left, 0, SUPER_ROWS), 0)
    sub_rows = sb_rows[:, None] - jnp.arange(SUB_PER_SUPER)[None, :] * ROW_BLOCK
    fill = (sub_rows < ROW_BLOCK).astype(jnp.int32).reshape(-1)
    xg = _dispatch(fill, dest3, n2)
    eo = _ffn(sb_expert.astype(jnp.int32), sb_nact.astype(jnp.int32), xg,
              w_gate_up.astype(BF16), w_down.astype(BF16),
              b_gate_up.reshape(N_EXPERTS, 1, 2 * D_FF), b_down.reshape(N_EXPERTS, 1, D_MODEL))
    out = _combine(dest3, top_w, h1, out_norm_w.reshape(1, -1), eo)
    return out.reshape(batch, seq, D_MODEL)


def kernel(x, norm_mix_w, in_proj_w, conv_w, conv_b, dt_bias, a_log, d_skip, ssd_norm_w,
           gate_up_w, gate_up_b, gla_norm_w, out_proj_w, norm_ffn_w, router_w, router_b,
           w_gate_up, b_gate_up, w_down, b_down, final_norm_w):
    assert norm_mix_w.shape[0] == 1
    return _layer(x, norm_mix_w[0], in_proj_w[0], conv_w[0], conv_b[0], dt_bias[0], a_log[0],
                  d_skip[0], ssd_norm_w[0], gate_up_w[0], gate_up_b[0], gla_norm_w[0],
                  out_proj_w[0], norm_ffn_w[0], router_w[0], router_b[0], w_gate_up[0],
                  b_gate_up[0], w_down[0], b_down[0], final_norm_w)
```

```python
import functools

import jax
import jax.numpy as jnp
from jax import lax
from jax.experimental import pallas as pl
from jax.experimental.pallas import tpu as pltpu

F32 = jnp.float32
BF16 = jnp.bfloat16

D_MODEL = 2048
NORM_EPS = 1e-5

SSD_WIDTH = 2048
SSD_HEAD_DIM = 64
SSD_HEADS = 32
SSD_GROUPS = 4
SSD_STATE = 128
CONV_WIDTH = 4
SSD_GROUP_WIDTH = SSD_WIDTH // SSD_GROUPS
SSD_BC_WIDTH = 2 * SSD_GROUPS * SSD_STATE
SSD_CONV_DIM = SSD_WIDTH + SSD_BC_WIDTH

GLA_HEADS = 4
GLA_KEY_WIDTH = 1024
GLA_VALUE_WIDTH = 2048
GLA_HEAD_K = 256
GLA_HEAD_V = 512
GLA_GATE_RANK = 16
GLA_GATE_NORMALIZER = 16.0

N_EXPERTS = 32
TOP_K = 4
D_FF = 2048
SWIGLU_LIMIT = 7.0
SWIGLU_ALPHA = 1.702

LANES = 128
SUBLANES = 8
VMEM_LIMIT_BYTES = 56 * 1024 * 1024

U_WIDTH = 11264
SMALL_WIDTH = LANES
SMALL_DT_OFF = 0
SMALL_GATE_OFF = SSD_HEADS

MIX_CHUNK = 128
GLA_SUB = 16
GLA_EXP_CLAMP = 80.0

ROW_BLOCK = 256
SUPER_ROWS = 1024
SUB_PER_SUPER = SUPER_ROWS // ROW_BLOCK
PACKED_WIDTH = D_MODEL // 2
FFN_TILE = 256
TOKEN_TILE = 256


def _split2(v):
    hi = v.astype(BF16)
    lo = (v - hi.astype(F32)).astype(BF16)
    return hi, lo


def _split3(v):
    hi = v.astype(BF16)
    r = v - hi.astype(F32)
    mid = r.astype(BF16)
    lo = (r - mid.astype(F32)).astype(BF16)
    return hi, mid, lo


def _dot(a, b):
    return jnp.dot(a, b, preferred_element_type=F32)


def _dot_nt(a, b):
    return lax.dot_general(a, b, (((1,), (1,)), ((), ())), preferred_element_type=F32)


def _dot_exact_lhs(mat01, v):
    hi, mid, lo = _split3(v)
    return _dot(mat01, hi) + _dot(mat01, mid) + _dot(mat01, lo)


def _dot_hilo(a, b_hi, b_lo):
    a_hi, a_lo = _split2(a)
    return _dot(a_hi, b_hi) + _dot(a_lo, b_hi) + _dot(a_hi, b_lo)


def _pack_bf16_pairs(v):
    c = v.shape[1] // 2
    lo = lax.bitcast_convert_type(v[:, :c].astype(BF16).astype(F32), jnp.uint32)
    hi = lax.bitcast_convert_type(v[:, c:].astype(BF16).astype(F32), jnp.uint32)
    return (lo >> 16) | (hi & jnp.uint32(0xFFFF0000))


def _unpack_bf16_pairs(w):
    lo = lax.bitcast_convert_type(w << 16, F32)
    hi = lax.bitcast_convert_type(w & jnp.uint32(0xFFFF0000), F32)
    return lo, hi


def _sigmoid(v):
    return 0.5 * jnp.tanh(0.5 * v) + 0.5


def _silu(v):
    return v * _sigmoid(v)


def _softplus(v):
    return jnp.maximum(v, 0.0) + jnp.log(1.0 + jnp.exp(-jnp.abs(v)))


def _log_sigmoid(v):
    return jnp.minimum(v, 0.0) - jnp.log(1.0 + jnp.exp(-jnp.abs(v)))


def _in_proj_kernel(offs_ref, x_ref, nw_ref, wt_ref, wsh_ref, wsl_ref, u_ref, small_ref,
                    n1_ref):
    del offs_ref
    @pl.when(pl.program_id(1) == 0)
    def _():
        x = x_ref[...]
        ms = jnp.mean(x * x, axis=-1, keepdims=True)
        n1 = x * lax.rsqrt(ms + NORM_EPS) * nw_ref[...]
        n1_ref[...] = n1.astype(BF16)
        n1_hi, n1_lo = _split2(n1)
        small_ref[...] = (_dot_nt(n1_hi, wsh_ref[...]) + _dot_nt(n1_lo, wsh_ref[...])
                          + _dot_nt(n1_hi, wsl_ref[...]))

    u_ref[...] = _dot_nt(n1_ref[...], wt_ref[...]).astype(BF16)


def _in_proj(row_offs, x2, norm_w, wt, ws_hi, ws_lo, *, tm, tn):
    m = x2.shape[0]
    grid_spec = pltpu.PrefetchScalarGridSpec(
        num_scalar_prefetch=1,
        grid=(m // tm, U_WIDTH // tn),
        in_specs=[
            pl.BlockSpec((tm, D_MODEL), lambda i, j, o: (i, 0)),
            pl.BlockSpec((1, D_MODEL), lambda i, j, o: (0, 0)),
            pl.BlockSpec((pl.Element(tn), pl.Element(D_MODEL)),
                         lambda i, j, o: (pl.multiple_of(o[j], 2 * SUBLANES), 0)),
            pl.BlockSpec((SMALL_WIDTH, D_MODEL), lambda i, j, o: (0, 0)),
            pl.BlockSpec((SMALL_WIDTH, D_MODEL), lambda i, j, o: (0, 0)),
        ],
        out_specs=[
            pl.BlockSpec((tm, tn), lambda i, j, o: (i, j)),
            pl.BlockSpec((tm, SMALL_WIDTH), lambda i, j, o: (i, 0)),
        ],
        scratch_shapes=[pltpu.VMEM((tm, D_MODEL), BF16)],
    )
    return pl.pallas_call(
        _in_proj_kernel,
        grid_spec=grid_spec,
        out_shape=[
            jax.ShapeDtypeStruct((m, U_WIDTH), BF16),
            jax.ShapeDtypeStruct((m, SMALL_WIDTH), F32),
        ],
        compiler_params=pltpu.CompilerParams(
            dimension_semantics=("arbitrary", "arbitrary"),
            vmem_limit_bytes=VMEM_LIMIT_BYTES),
        name="in_proj",
    )(row_offs, x2, norm_w, wt, ws_hi, ws_lo)


def _ssd_kernel(z_ref, x_ref, bc_ref, sm_ref, convw_ref, convb_ref, dtb_ref, alog_ref,
                dsk_ref, nw_ref, e2_ref, tri_ref, shift_ref, o_ref,
                ext_ref, xc_ref, xd_ref, xdd_ref, y_ref, state_ref):
    L = MIX_CHUNK
    G, N, P = SSD_GROUPS, SSD_STATE, SSD_HEAD_DIM
    GW = SSD_GROUP_WIDTH

    @pl.when(pl.program_id(1) == 0)
    def _():
        ext_ref[0:L, :] = jnp.zeros((L, SSD_CONV_DIM), BF16)
        state_ref[...] = jnp.zeros(state_ref.shape, F32)

    ext_ref[L:2 * L, 0:SSD_WIDTH] = x_ref[...]
    ext_ref[L:2 * L, SSD_WIDTH:SSD_CONV_DIM] = bc_ref[...]
    slab = 512
    for s in range(SSD_CONV_DIM // slab):
        cols = slice(s * slab, (s + 1) * slab)
        acc = (convb_ref[:, cols]
               + convw_ref[CONV_WIDTH - 1:CONV_WIDTH, cols] * ext_ref[L:2 * L, cols].astype(F32))
        for k in range(1, CONV_WIDTH):
            w = CONV_WIDTH - 1 - k
            acc = acc + convw_ref[w:w + 1, cols] * _dot(shift_ref[k - 1], ext_ref[:, cols])
        xc_ref[:, cols] = _silu(acc)
    ext_ref[0:L, :] = ext_ref[L:2 * L, :]

    lane = lax.broadcasted_iota(jnp.int32, (1, LANES), 1)
    a = jnp.where(lane < SSD_HEADS, -jnp.exp(alog_ref[...]), 0.0)
    dt = _softplus(sm_ref[...] + dtb_ref[...])
    la = _dot_exact_lhs(tri_ref[...], dt * a)
    la_t = la.T
    la_end = la[L - 1:L, :]

    def hilo(v):
        hi, lo = _split2(v)
        return jnp.concatenate([hi, lo], axis=1)

    def expand(v_hl, cols):
        return _dot(v_hl, e2_ref[:, cols])

    dt_hl = hilo(dt)
    ela_hl = hilo(jnp.exp(la))
    dte_hl = hilo(jnp.exp(la_end - la))
    cd_hl = hilo(jnp.broadcast_to(jnp.exp(la_end), (SUBLANES, LANES)))

    for s in range(SSD_WIDTH // slab):
        cols = slice(s * slab, (s + 1) * slab)
        xd = xc_ref[:, cols] * expand(dt_hl, cols)
        xd_ref[:, cols] = xd
        xdd_ref[:, cols] = (xd * expand(dte_hl, cols)).astype(BF16)

    row = lax.broadcasted_iota(jnp.int32, (L, L), 0)
    col = lax.broadcasted_iota(jnp.int32, (L, L), 1)
    causal = row >= col
    lane_l = lax.broadcasted_iota(jnp.int32, (L, LANES), 1)

    for g in range(G):
        bm = xc_ref[:, SSD_WIDTH + g * N:SSD_WIDTH + (g + 1) * N]
        cm = xc_ref[:, SSD_WIDTH + G * N + g * N:SSD_WIDTH + G * N + (g + 1) * N]
        cm_b = cm.astype(BF16)
        cb = _dot_nt(cm_b, bm.astype(BF16))
        gcols = slice(g * GW, (g + 1) * GW)
        y_ref[:, gcols] = (_dot(cm_b, state_ref[g].astype(BF16)) * expand(ela_hl, gcols)
                           + dsk_ref[:, gcols] * xc_ref[:, gcols])
        for p in range(GW // LANES):
            h0 = g * (GW // P) + 2 * p
            ms = []
            for h in (h0, h0 + 1):
                seg = la[:, h:h + 1] - la_t[h:h + 1, :]
                dec = jnp.where(causal, jnp.exp(jnp.minimum(seg, 0.0)), 0.0)
                ms.append((cb * dec).astype(BF16))
            pcols = slice(g * GW + p * LANES, g * GW + (p + 1) * LANES)
            xp = xd_ref[:, pcols]
            rhs = jnp.concatenate([jnp.where(lane_l < P, xp, 0.0),
                                   jnp.where(lane_l >= P, xp, 0.0)], axis=0).astype(BF16)
            y_ref[:, pcols] = y_ref[:, pcols] + _dot(jnp.concatenate(ms, axis=1), rhs)
        upd = _dot(bm.T.astype(BF16), xdd_ref[:, gcols])
        state_ref[g] = expand(cd_hl, gcols)[0:1, :] * state_ref[g] + upd

    for g in range(G):
        gcols = slice(g * GW, (g + 1) * GW)
        y = y_ref[:, gcols] * _silu(z_ref[:, gcols].astype(F32))
        ms = jnp.mean(y * y, axis=-1, keepdims=True)
        o_ref[:, gcols] = (y * lax.rsqrt(ms + NORM_EPS) * nw_ref[:, gcols]).astype(BF16)


def _ssd(u, small, conv_w, conv_b, dtb, alog, dsk_e, norm_w, e2, tri, *, batch, seq):
    L = MIX_CHUNK
    nc = seq // L
    m = batch * seq
    t_idx = jnp.arange(L)[None, :, None]
    j_idx = jnp.arange(2 * L)[None, None, :]
    k_idx = jnp.arange(1, CONV_WIDTH)[:, None, None]
    shift = (j_idx == L + t_idx - k_idx).astype(BF16)

    def rows(b, c):
        return b * nc + c

    const = lambda shape: pl.BlockSpec(shape, lambda b, c: (0,) * len(shape))
    return pl.pallas_call(
        _ssd_kernel,
        grid=(batch, nc),
        in_specs=[
            pl.BlockSpec((L, SSD_WIDTH), lambda b, c: (rows(b, c), 0)),
            pl.BlockSpec((L, SSD_WIDTH), lambda b, c: (rows(b, c), 1)),
            pl.BlockSpec((L, SSD_BC_WIDTH), lambda b, c: (rows(b, c), 8)),
            pl.BlockSpec((L, SMALL_WIDTH), lambda b, c: (rows(b, c), 0)),
            const((CONV_WIDTH, SSD_CONV_DIM)),
            const((1, SSD_CONV_DIM)),
            const((1, LANES)),
            const((1, LANES)),
            const((1, SSD_WIDTH)),
            const((1, SSD_WIDTH)),
            const((2 * LANES, SSD_WIDTH)),
            const((L, L)),
            const((CONV_WIDTH - 1, L, 2 * L)),
        ],
        out_specs=pl.BlockSpec((L, SSD_WIDTH), lambda b, c: (rows(b, c), 0)),
        out_shape=jax.ShapeDtypeStruct((m, SSD_WIDTH), BF16),
        scratch_shapes=[
            pltpu.VMEM((2 * L, SSD_CONV_DIM), BF16),
            pltpu.VMEM((L, SSD_CONV_DIM), F32),
            pltpu.VMEM((L, SSD_WIDTH), F32),
            pltpu.VMEM((L, SSD_WIDTH), BF16),
            pltpu.VMEM((L, SSD_WIDTH), F32),
            pltpu.VMEM((SSD_GROUPS, SSD_STATE, SSD_GROUP_WIDTH), F32),
        ],
        compiler_params=pltpu.CompilerParams(
            dimension_semantics=("arbitrary", "arbitrary"),
            vmem_limit_bytes=VMEM_LIMIT_BYTES),
        name="ssd",
    )(u, u, u, small, conv_w, conv_b, dtb, alog, dsk_e, norm_w, e2, tri, shift)


def _gla_kernel(q_ref, k_ref, v_ref, g_ref, sm_ref, wuh_ref, wul_ref, bup_ref, nw_ref,
                tri_ref, o_ref, cum_ref, s_ref, st_ref):
    L, C = MIX_CHUNK, GLA_SUB
    K, V = GLA_HEAD_K, GLA_HEAD_V

    @pl.when(pl.program_id(1) == 0)
    def _():
        st_ref[...] = jnp.zeros(st_ref.shape, F32)

    pre = _dot_hilo(sm_ref[...], wuh_ref[...], wul_ref[...]) + bup_ref[...]
    gk = _log_sigmoid(pre) * (1.0 / GLA_GATE_NORMALIZER)
    cum_ref[...] = _dot_exact_lhs(tri_ref[...], gk)

    row_c = lax.broadcasted_iota(jnp.int32, (C, L), 0)
    col_c = lax.broadcasted_iota(jnp.int32, (C, L), 1)
    for h in range(GLA_HEADS):
        ks = slice(h * K, (h + 1) * K)
        vs = slice(h * V, (h + 1) * V)
        cum = cum_ref[:, ks]
        cum_last = cum_ref[L - 1:L, ks]
        qh = q_ref[:, ks].astype(F32) * (K ** -0.5)
        kh = k_ref[:, ks].astype(F32)
        vh = v_ref[:, vs]
        for blk in range(L // C):
            rows = slice(blk * C, (blk + 1) * C)
            if blk == 0:
                ref_row = jnp.zeros((1, K), F32)
            else:
                ref_row = cum_ref[blk * C - 1:blk * C, ks]
            q_b = (qh[rows] * jnp.exp(cum_ref[rows, ks] - ref_row)).astype(BF16)
            n = (blk + 1) * C
            k_b = (kh[0:n] * jnp.exp(jnp.minimum(ref_row - cum[0:n], GLA_EXP_CLAMP))).astype(BF16)
            if n < L:
                k_b = jnp.concatenate([k_b, jnp.zeros((L - n, K), BF16)], axis=0)
            s_b = _dot_nt(q_b, k_b)
            s_ref[rows, :] = jnp.where(col_c <= row_c + blk * C, s_b, 0.0)
        o_intra = _dot(s_ref[...].astype(BF16), vh)
        st = st_ref[h]
        o_inter = _dot_nt((qh * jnp.exp(cum)).astype(BF16), st.astype(BF16))
        k_dec = (kh * jnp.exp(cum_last - cum)).astype(BF16)
        st_ref[h] = st * jnp.exp(cum_last) + _dot(vh.astype(F32).T.astype(BF16), k_dec)
        o = o_inter + o_intra
        ms = jnp.mean(o * o, axis=-1, keepdims=True)
        o = o * lax.rsqrt(ms + NORM_EPS) * nw_ref[...] * _silu(g_ref[:, vs].astype(F32))
        o_ref[:, vs] = o.astype(BF16)


def _gla(u, small, wup_hi, wup_lo, bup, norm_w, tri, *, batch, seq):
    L = MIX_CHUNK
    nc = seq // L
    m = batch * seq

    def rows(b, c):
        return b * nc + c

    const = lambda shape: pl.BlockSpec(shape, lambda b, c: (0,) * len(shape))
    return pl.pallas_call(
        _gla_kernel,
        grid=(batch, nc),
        in_specs=[
            pl.BlockSpec((L, GLA_KEY_WIDTH), lambda b, c: (rows(b, c), 9)),
            pl.BlockSpec((L, GLA_KEY_WIDTH), lambda b, c: (rows(b, c), 10)),
            pl.BlockSpec((L, GLA_VALUE_WIDTH), lambda b, c: (rows(b, c), 2)),
            pl.BlockSpec((L, GLA_VALUE_WIDTH), lambda b, c: (rows(b, c), 3)),
            pl.BlockSpec((L, SMALL_WIDTH), lambda b, c: (rows(b, c), 0)),
            const((SMALL_WIDTH, GLA_KEY_WIDTH)),
            const((SMALL_WIDTH, GLA_KEY_WIDTH)),
            const((1, GLA_KEY_WIDTH)),
            const((1, GLA_HEAD_V)),
            const((L, L)),
        ],
        out_specs=pl.BlockSpec((L, GLA_VALUE_WIDTH), lambda b, c: (rows(b, c), 0)),
        out_shape=jax.ShapeDtypeStruct((m, GLA_VALUE_WIDTH), BF16),
        scratch_shapes=[
            pltpu.VMEM((L, GLA_KEY_WIDTH), F32),
            pltpu.VMEM((L, L), F32),
            pltpu.VMEM((GLA_HEADS, GLA_HEAD_V, GLA_HEAD_K), F32),
        ],
        compiler_params=pltpu.CompilerParams(
            dimension_semantics=("arbitrary", "arbitrary"),
            vmem_limit_bytes=VMEM_LIMIT_BYTES),
        name="gla",
    )(u, u, u, u, small, wup_hi, wup_lo, bup, norm_w, tri)


def _out_proj_kernel(ys_ref, yg_ref, x_ref, wo1_ref, wo2_ref, nw_ref, wrh_ref, wrl_ref,
                     rb_ref, stri_ref, h1_ref, n2_ref, idx_ref, wt_ref, rank_ref, cnt_ref,
                     run_ref):
    tm = x_ref.shape[0]

    @pl.when(pl.program_id(0) == 0)
    def _():
        run_ref[...] = jnp.zeros(run_ref.shape, F32)

    h = x_ref[...] + _dot(ys_ref[...], wo1_ref[...]) + _dot(yg_ref[...], wo2_ref[...])
    h1_ref[...] = h
    ms = jnp.mean(h * h, axis=-1, keepdims=True)
    n2 = h * lax.rsqrt(ms + NORM_EPS) * nw_ref[...]
    n2_ref[...] = _pack_bf16_pairs(n2)

    lane = lax.broadcasted_iota(jnp.int32, (tm, LANES), 1)
    lane_f = lane.astype(F32)
    logits = _dot_hilo(n2, wrh_ref[...], wrl_ref[...]) + rb_ref[...]
    work = jnp.where(lane < N_EXPERTS, logits, -jnp.inf)
    vals, idxs, sels = [], [], []
    for _ in range(TOP_K):
        top = jnp.max(work, axis=-1, keepdims=True)
        idx = jnp.min(jnp.where(work == top, lane_f, float(LANES)), axis=-1, keepdims=True)
        sel = lane_f == idx
        vals.append(top)
        idxs.append(idx)
        sels.append(sel)
        work = jnp.where(sel, -jnp.inf, work)
    exps = [jnp.exp(v - vals[0]) for v in vals]
    denom = exps[0] + exps[1] + exps[2] + exps[3]

    chosen = jnp.zeros((tm, LANES), F32)
    for sel in sels:
        chosen = chosen + jnp.where(sel, 1.0, 0.0)
    before = run_ref[0:1, :] + _dot(stri_ref[...], chosen.astype(BF16))

    idx_out = jnp.zeros((tm, LANES), F32)
    wt_out = jnp.zeros((tm, LANES), F32)
    rank_out = jnp.zeros((tm, LANES), F32)
    for j in range(TOP_K):
        rank_j = jnp.sum(jnp.where(sels[j], before, 0.0), axis=-1, keepdims=True)
        idx_out = jnp.where(lane == j, idxs[j], idx_out)
        wt_out = jnp.where(lane == j, exps[j] / denom, wt_out)
        rank_out = jnp.where(lane == j, rank_j, rank_out)
    idx_ref[...] = idx_out.astype(jnp.int32)
    wt_ref[...] = wt_out
    rank_ref[...] = rank_out.astype(jnp.int32)

    run_ref[...] = run_ref[...] + jnp.sum(chosen, axis=0, keepdims=True)
    cnt_ref[...] = run_ref[...].astype(jnp.int32)


def _out_proj(y_ssd, y_gla, x2, wo1, wo2, norm_w, wr_hi, wr_lo, rb, stri, *, tm):
    m = x2.shape[0]
    const = lambda shape: pl.BlockSpec(shape, lambda i: (0,) * len(shape))
    tile = lambda w: pl.BlockSpec((tm, w), lambda i: (i, 0))
    return pl.pallas_call(
        _out_proj_kernel,
        grid=(m // tm,),
        in_specs=[
            tile(SSD_WIDTH), tile(GLA_VALUE_WIDTH), tile(D_MODEL),
            const((SSD_WIDTH, D_MODEL)), const((GLA_VALUE_WIDTH, D_MODEL)),
            const((1, D_MODEL)),
            const((D_MODEL, LANES)), const((D_MODEL, LANES)), const((1, LANES)),
            const((tm, tm)),
        ],
        out_specs=[
            tile(D_MODEL), tile(PACKED_WIDTH), tile(LANES), tile(LANES), tile(LANES),
            const((SUBLANES, LANES)),
        ],
        out_shape=[
            jax.ShapeDtypeStruct((m, D_MODEL), F32),
            jax.ShapeDtypeStruct((m, PACKED_WIDTH), jnp.uint32),
            jax.ShapeDtypeStruct((m, LANES), jnp.int32),
            jax.ShapeDtypeStruct((m, LANES), F32),
            jax.ShapeDtypeStruct((m, LANES), jnp.int32),
            jax.ShapeDtypeStruct((SUBLANES, LANES), jnp.int32),
        ],
        scratch_shapes=[pltpu.VMEM((SUBLANES, LANES), F32)],
        compiler_params=pltpu.CompilerParams(
            dimension_semantics=("arbitrary",),
            vmem_limit_bytes=VMEM_LIMIT_BYTES),
        name="out_proj",
    )(y_ssd, y_gla, x2, wo1, wo2, norm_w, wr_hi, wr_lo, rb, stri)


def _dispatch_kernel(fill_ref, dest_ref, n2_ref, xg_hbm, zero_ref, sem, zsem):
    tm = TOKEN_TILE

    @pl.when(pl.program_id(0) == 0)
    def _():
        zero_ref[...] = jnp.zeros(zero_ref.shape, zero_ref.dtype)

        def fill(b):
            row = pl.multiple_of(b * ROW_BLOCK, ROW_BLOCK)
            return pltpu.make_async_copy(zero_ref, xg_hbm.at[pl.ds(row, ROW_BLOCK)], zsem)

        def start_fill(b, carry):
            @pl.when(fill_ref[b] > 0)
            def _():
                fill(b).start()
            return carry

        def wait_fill(b, carry):
            @pl.when(fill_ref[b] > 0)
            def _():
                fill(b).wait()
            return carry

        lax.fori_loop(0, fill_ref.shape[0], start_fill, 0)
        lax.fori_loop(0, fill_ref.shape[0], wait_fill, 0)

    def copy(r, row):
        return pltpu.make_async_copy(n2_ref.at[pl.ds(r, 1)], xg_hbm.at[pl.ds(row, 1)], sem)

    for r in range(tm):
        for j in range(TOP_K):
            copy(r, dest_ref[0, 0, TOP_K * r + j]).start(priority=j % 2)
    for r in range(tm):
        for j in range(TOP_K):
            copy(r, 0).wait()


def _dispatch(fill, dest3, n2p):
    n_tiles = dest3.shape[0]
    n_rows = fill.shape[0] * ROW_BLOCK
    grid_spec = pltpu.PrefetchScalarGridSpec(
        num_scalar_prefetch=1,
        grid=(n_tiles,),
        in_specs=[
            pl.BlockSpec((1, 1, TOP_K * TOKEN_TILE), lambda i, fl: (i, 0, 0),
                         memory_space=pltpu.SMEM),
            pl.BlockSpec((TOKEN_TILE, PACKED_WIDTH), lambda i, fl: (i, 0)),
        ],
        out_specs=pl.BlockSpec(memory_space=pl.ANY),
        scratch_shapes=[
            pltpu.VMEM((ROW_BLOCK, PACKED_WIDTH), jnp.uint32),
            pltpu.SemaphoreType.DMA(()),
            pltpu.SemaphoreType.DMA(()),
        ],
    )
    return pl.pallas_call(
        _dispatch_kernel,
        grid_spec=grid_spec,
        out_shape=jax.ShapeDtypeStruct((n_rows, PACKED_WIDTH), jnp.uint32),
        compiler_params=pltpu.CompilerParams(
            dimension_semantics=("arbitrary",), has_side_effects=True),
        name="dispatch",
    )(fill, dest3, n2p)


def _ffn_kernel(sbe_ref, nact_ref, x_ref, wg_ref, wu_ref, wd_ref, bg_ref, bu_ref, bd_ref,
                o_ref, xb_ref, act_ref, wgb_ref, wub_ref, wdb_ref):
    del sbe_ref
    i = pl.program_id(0)
    f = pl.program_id(1)
    nf = D_FF // FFN_TILE
    nact = nact_ref[i]

    full = nact == SUB_PER_SUPER

    def gate_up(sb):
        rows = slice(sb * ROW_BLOCK, (sb + 1) * ROW_BLOCK)
        xs = xb_ref[rows, :]
        gate = jnp.minimum(_dot(xs, wgb_ref[...]) + bg_ref[0], SWIGLU_LIMIT)
        up = jnp.clip(_dot(xs, wub_ref[...]) + bu_ref[0], -SWIGLU_LIMIT, SWIGLU_LIMIT)
        act_ref[f, rows, :] = ((up + 1.0) * gate * _sigmoid(SWIGLU_ALPHA * gate)).astype(BF16)

    def down(sb):
        rows = slice(sb * ROW_BLOCK, (sb + 1) * ROW_BLOCK)
        act = jnp.concatenate([act_ref[k, rows, :] for k in range(nf)], axis=1)
        out = _dot(act, wdb_ref[...]) + bd_ref[0]
        o_ref[rows, :] = _pack_bf16_pairs(out)

    @pl.when(jnp.logical_and(nact > 0, f == 0))
    def _():
        for sb in range(SUB_PER_SUPER):
            rows = slice(sb * ROW_BLOCK, (sb + 1) * ROW_BLOCK)
            lo, hi = _unpack_bf16_pairs(x_ref[rows, :])
            xb_ref[rows, 0:PACKED_WIDTH] = lo.astype(BF16)
            xb_ref[rows, PACKED_WIDTH:D_MODEL] = hi.astype(BF16)

    def cast_weights():
        wgb_ref[...] = wg_ref[0].astype(BF16)
        wub_ref[...] = wu_ref[0].astype(BF16)
        wdb_ref[pl.ds(pl.multiple_of(f * FFN_TILE, FFN_TILE), FFN_TILE), :] = (
            wd_ref[0].astype(BF16))

    @pl.when(full)
    def _():
        cast_weights()
        for sb in range(SUB_PER_SUPER):
            gate_up(sb)

    @pl.when(jnp.logical_and(nact > 0, jnp.logical_not(full)))
    def _():
        cast_weights()
        gate_up(0)

    for sb in range(1, SUB_PER_SUPER - 1):
        @pl.when(jnp.logical_and(sb < nact, jnp.logical_not(full)))
        def _():
            gate_up(sb)

    @pl.when(f == nf - 1)
    def _():
        @pl.when(full)
        def _():
            for sb in range(SUB_PER_SUPER):
                down(sb)

        for sb in range(SUB_PER_SUPER):
            rows = slice(sb * ROW_BLOCK, (sb + 1) * ROW_BLOCK)

            @pl.when(jnp.logical_and(sb < nact, jnp.logical_not(full)))
            def _():
                down(sb)

            @pl.when(sb >= nact)
            def _():
                o_ref[rows, :] = jnp.zeros((ROW_BLOCK, PACKED_WIDTH), jnp.uint32)


def _ffn(sb_expert, sb_nact, xg, wgu, wd, bgu3, bd3):
    n_super = sb_expert.shape[0]
    nf = D_FF // FFN_TILE
    grid_spec = pltpu.PrefetchScalarGridSpec(
        num_scalar_prefetch=2,
        grid=(n_super, nf),
        in_specs=[
            pl.BlockSpec((SUPER_ROWS, PACKED_WIDTH), lambda i, f, e, n: (i, 0)),
            pl.BlockSpec((1, D_MODEL, FFN_TILE), lambda i, f, e, n: (e[i], 0, f)),
            pl.BlockSpec((1, D_MODEL, FFN_TILE), lambda i, f, e, n: (e[i], 0, f + nf)),
            pl.BlockSpec((1, FFN_TILE, D_MODEL), lambda i, f, e, n: (e[i], f, 0)),
            pl.BlockSpec((1, 1, FFN_TILE), lambda i, f, e, n: (e[i], 0, f)),
            pl.BlockSpec((1, 1, FFN_TILE), lambda i, f, e, n: (e[i], 0, f + nf)),
            pl.BlockSpec((1, 1, D_MODEL), lambda i, f, e, n: (e[i], 0, 0)),
        ],
        out_specs=pl.BlockSpec((SUPER_ROWS, PACKED_WIDTH), lambda i, f, e, n: (i, 0)),
        scratch_shapes=[
            pltpu.VMEM((SUPER_ROWS, D_MODEL), BF16),
            pltpu.VMEM((D_FF // FFN_TILE, SUPER_ROWS, FFN_TILE), BF16),
            pltpu.VMEM((D_MODEL, FFN_TILE), BF16),
            pltpu.VMEM((D_MODEL, FFN_TILE), BF16),
            pltpu.VMEM((D_FF, D_MODEL), BF16),
        ],
    )
    return pl.pallas_call(
        _ffn_kernel,
        grid_spec=grid_spec,
        out_shape=jax.ShapeDtypeStruct(xg.shape, jnp.uint32),
        compiler_params=pltpu.CompilerParams(
            dimension_semantics=("arbitrary", "arbitrary"),
            vmem_limit_bytes=VMEM_LIMIT_BYTES),
        name="ffn",
    )(sb_expert, sb_nact, xg, wgu, wgu, wd, bgu3, bgu3, bd3)


def _combine_kernel(dcur_ref, dnxt_ref, wt_ref, h1_ref, fw_ref, eo_hbm, o_ref, buf, sem):
    tm = TOKEN_TILE
    i = pl.program_id(0)
    n = pl.num_programs(0)
    slot = lax.rem(i, 2)

    def copy(row, s, r, j):
        return pltpu.make_async_copy(
            eo_hbm.at[pl.ds(row, 1)], buf.at[s, j, pl.ds(r, 1)], sem.at[s])

    def start_all(dref, s):
        for r in range(tm):
            for j in range(TOP_K):
                copy(dref[0, 0, TOP_K * r + j], s, r, j).start(priority=j % 2)

    def wait_all(s):
        for r in range(tm):
            for j in range(TOP_K):
                copy(0, s, r, j).wait()

    @pl.when(i == 0)
    def _():
        start_all(dcur_ref, 0)

    for s in range(2):
        @pl.when(jnp.logical_and(i + 1 < n, slot != s))
        def _():
            start_all(dnxt_ref, s)

    for s in range(2):
        @pl.when(slot == s)
        def _():
            wait_all(s)

    wt = wt_ref[...]
    half = PACKED_WIDTH
    y_lo = h1_ref[:, 0:half]
    y_hi = h1_ref[:, half:D_MODEL]
    for j in range(TOP_K):
        lo, hi = _unpack_bf16_pairs(buf[slot, j])
        y_lo = y_lo + lo * wt[:, j:j + 1]
        y_hi = y_hi + hi * wt[:, j:j + 1]
    ssq = (jnp.sum(y_lo * y_lo, axis=-1, keepdims=True)
           + jnp.sum(y_hi * y_hi, axis=-1, keepdims=True))
    scale = lax.rsqrt(ssq * (1.0 / D_MODEL) + NORM_EPS)
    o_ref[:, 0:half] = y_lo * scale * fw_ref[:, 0:half]
    o_ref[:, half:D_MODEL] = y_hi * scale * fw_ref[:, half:D_MODEL]


def _combine(dest3, wt, h1, final_w, eo):
    n_tiles = dest3.shape[0]
    tm = TOKEN_TILE
    m = h1.shape[0]
    smem = lambda fn: pl.BlockSpec((1, 1, TOP_K * tm), fn, memory_space=pltpu.SMEM)
    return pl.pallas_call(
        _combine_kernel,
        grid=(n_tiles,),
        in_specs=[
            smem(lambda i: (i, 0, 0)),
            smem(lambda i: (jnp.minimum(i + 1, n_tiles - 1), 0, 0)),
            pl.BlockSpec((tm, LANES), lambda i: (i, 0)),
            pl.BlockSpec((tm, D_MODEL), lambda i: (i, 0)),
            pl.BlockSpec((1, D_MODEL), lambda i: (0, 0)),
            pl.BlockSpec(memory_space=pl.ANY),
        ],
        out_specs=pl.BlockSpec((tm, D_MODEL), lambda i: (i, 0)),
        out_shape=jax.ShapeDtypeStruct((m, D_MODEL), F32),
        scratch_shapes=[
            pltpu.VMEM((2, TOP_K, tm, PACKED_WIDTH), jnp.uint32),
            pltpu.SemaphoreType.DMA((2,)),
        ],
        compiler_params=pltpu.CompilerParams(
            dimension_semantics=("arbitrary",),
            vmem_limit_bytes=VMEM_LIMIT_BYTES),
        name="combine",
    )(dest3, dest3, wt, h1, final_w, eo)


def _pad_lanes(v, width=LANES):
    return jnp.pad(v, [(0, 0)] * (v.ndim - 1) + [(0, width - v.shape[-1])])


def _layer(h, norm_mix_w, in_proj_w, conv_w, conv_b, dt_bias, a_log, d_skip, ssd_norm_w,
           gate_up_w, gate_up_b, gla_norm_w, out_proj_w, norm_ffn_w, router_w, router_b,
           w_gate_up, b_gate_up, w_down, b_down, out_norm_w):
    batch, seq, _ = h.shape
    m = batch * seq
    x2 = h.reshape(m, D_MODEL)
    L = MIX_CHUNK

    w = in_proj_w.T
    o_xbc = SSD_WIDTH
    o_dt = o_xbc + SSD_CONV_DIM
    o_q = o_dt + SSD_HEADS
    o_k = o_q + GLA_KEY_WIDTH
    o_v = o_k + GLA_KEY_WIDTH
    o_g = o_v + GLA_VALUE_WIDTH
    o_gl = o_g + GLA_VALUE_WIDTH
    tn = 1024
    seg_starts = (0, o_xbc, o_v, o_g, o_xbc + SSD_WIDTH, o_q, o_k)
    seg_widths = (SSD_WIDTH, SSD_WIDTH, GLA_VALUE_WIDTH, GLA_VALUE_WIDTH, SSD_BC_WIDTH,
                  GLA_KEY_WIDTH, GLA_KEY_WIDTH)
    row_offs = jnp.array([s + t for s, wd_ in zip(seg_starts, seg_widths)
                          for t in range(0, wd_, tn)], jnp.int32)
    n_small = SSD_HEADS + GLA_GATE_RANK
    wt_small = jnp.concatenate([w[o_dt:o_q], w[o_gl:o_gl + GLA_GATE_RANK],
                                jnp.zeros((SMALL_WIDTH - n_small, D_MODEL), F32)], axis=0)
    ws_hi, ws_lo = _split2(wt_small)

    u, small = _in_proj(row_offs, x2, norm_mix_w.reshape(1, D_MODEL), w.astype(BF16),
                        ws_hi, ws_lo, tm=min(1024, m), tn=tn)

    tri = jnp.tril(jnp.ones((L, L), F32)).astype(BF16)
    head_of_col = jnp.arange(SSD_WIDTH) // SSD_HEAD_DIM
    expand = (jnp.arange(LANES)[:, None] == head_of_col[None, :]).astype(BF16)
    e2 = jnp.concatenate([expand, expand], axis=0)
    y_ssd = _ssd(u, small, conv_w, conv_b.reshape(1, -1),
                 _pad_lanes(dt_bias.reshape(1, -1)), _pad_lanes(a_log.reshape(1, -1)),
                 jnp.repeat(d_skip, SSD_HEAD_DIM).reshape(1, -1), ssd_norm_w.reshape(1, -1),
                 e2, tri, batch=batch, seq=seq)

    wup = jnp.zeros((SMALL_WIDTH, GLA_KEY_WIDTH), F32)
    wup = wup.at[SMALL_GATE_OFF:SMALL_GATE_OFF + GLA_GATE_RANK].set(gate_up_w)
    wup_hi, wup_lo = _split2(wup)
    y_gla = _gla(u, small, wup_hi, wup_lo, gate_up_b.reshape(1, -1), gla_norm_w.reshape(1, -1),
                 tri, batch=batch, seq=seq)

    tm = TOKEN_TILE
    wr_hi, wr_lo = _split2(_pad_lanes(router_w))
    stri = jnp.tril(jnp.ones((tm, tm), F32), k=-1).astype(BF16)
    wo = out_proj_w.astype(BF16)
    h1, n2, top_idx, top_w, rank, cnt = _out_proj(
        y_ssd, y_gla, x2, wo[:SSD_WIDTH], wo[SSD_WIDTH:], norm_ffn_w.reshape(1, -1),
        wr_hi, wr_lo, _pad_lanes(router_b.reshape(1, -1)), stri, tm=tm)

    counts = cnt[0, :N_EXPERTS]
    n_sb = (counts + SUPER_ROWS - 1) // SUPER_ROWS
    sb_end = jnp.cumsum(n_sb)
    sb_start = sb_end - n_sb
    dest = (sb_start * SUPER_ROWS)[top_idx[:, :TOP_K]] + rank[:, :TOP_K]
    dest3 = dest.astype(jnp.int32).reshape(m // tm, 1, TOP_K * tm)

    n_super = (m * TOP_K + N_EXPERTS * (SUPER_ROWS - 1)) // SUPER_ROWS
    sb_ids = jnp.arange(n_super)
    total_sb = sb_end[-1]
    sb_live = jnp.minimum(sb_ids, total_sb - 1)
    sb_expert = jnp.minimum(jnp.searchsorted(sb_end, sb_live, side="right"), N_EXPERTS - 1)
    rows_left = counts[sb_expert] - (sb_live - sb_start[sb_expert]) * SUPER_ROWS
    sb_nact = jnp.where(sb_ids < total_sb,
                        (jnp.clip(rows_left, 0, SUPER_ROWS) + ROW_BLOCK - 1) // ROW_BLOCK, 0)

    sb_rows = jnp.where(sb_ids < total_sb, jnp.clip(rows_left, 0, SUPER_ROWS), 0)
    sub_rows = sb_rows[:, None] - jnp.arange(SUB_PER_SUPER)[None, :] * ROW_BLOCK
    fill = (sub_rows < ROW_BLOCK).astype(jnp.int32).reshape(-1)
    xg = _dispatch(fill, dest3, n2)
    eo = _ffn(sb_expert.astype(jnp.int32), sb_nact.astype(jnp.int32), xg,
              w_gate_up, w_down,
              b_gate_up.reshape(N_EXPERTS, 1, 2 * D_FF), b_down.reshape(N_EXPERTS, 1, D_MODEL))
    out = _combine(dest3, top_w, h1, out_norm_w.reshape(1, -1), eo)
    return out.reshape(batch, seq, D_MODEL)


def kernel(x, norm_mix_w, in_proj_w, conv_w, conv_b, dt_bias, a_log, d_skip, ssd_norm_w,
           gate_up_w, gate_up_b, gla_norm_w, out_proj_w, norm_ffn_w, router_w, router_b,
           w_gate_up, b_gate_up, w_down, b_down, final_norm_w):
    assert norm_mix_w.shape[0] == 1
    return _layer(x, norm_mix_w[0], in_proj_w[0], conv_w[0], conv_b[0], dt_bias[0], a_log[0],
                  d_skip[0], ssd_norm_w[0], gate_up_w[0], gate_up_b[0], gla_norm_w[0],
                  out_proj_w[0], norm_ffn_w[0], router_w[0], router_b[0], w_gate_up[0],
                  b_gate_up[0], w_down[0], b_down[0], final_norm_w)
```

```python
import functools

import jax
import jax.numpy as jnp
from jax import lax
from jax.experimental import pallas as pl
from jax.experimental.pallas import tpu as pltpu

F32 = jnp.float32
BF16 = jnp.bfloat16

D_MODEL = 2048
NORM_EPS = 1e-5

SSD_WIDTH = 2048
SSD_HEAD_DIM = 64
SSD_HEADS = 32
SSD_GROUPS = 4
SSD_STATE = 128
CONV_WIDTH = 4
SSD_GROUP_WIDTH = SSD_WIDTH // SSD_GROUPS
SSD_BC_WIDTH = 2 * SSD_GROUPS * SSD_STATE
SSD_CONV_DIM = SSD_WIDTH + SSD_BC_WIDTH

GLA_HEADS = 4
GLA_KEY_WIDTH = 1024
GLA_VALUE_WIDTH = 2048
GLA_HEAD_K = 256
GLA_HEAD_V = 512
GLA_GATE_RANK = 16
GLA_GATE_NORMALIZER = 16.0

N_EXPERTS = 32
TOP_K = 4
D_FF = 2048
SWIGLU_LIMIT = 7.0
SWIGLU_ALPHA = 1.702

LANES = 128
SUBLANES = 8
VMEM_LIMIT_BYTES = 56 * 1024 * 1024

U_WIDTH = 11264
SMALL_WIDTH = LANES
SMALL_DT_OFF = 0
SMALL_GATE_OFF = SSD_HEADS

MIX_CHUNK = 128
GLA_SUB = 16
GLA_EXP_CLAMP = 80.0

ROW_BLOCK = 256
SUPER_ROWS = 768
SUB_PER_SUPER = SUPER_ROWS // ROW_BLOCK
PACKED_WIDTH = D_MODEL // 2
FFN_TILE = 512
TOKEN_TILE = 256


def _split2(v):
    hi = v.astype(BF16)
    lo = (v - hi.astype(F32)).astype(BF16)
    return hi, lo


def _split3(v):
    hi = v.astype(BF16)
    r = v - hi.astype(F32)
    mid = r.astype(BF16)
    lo = (r - mid.astype(F32)).astype(BF16)
    return hi, mid, lo


def _dot(a, b):
    return jnp.dot(a, b, preferred_element_type=F32)


def _dot_nt(a, b):
    return lax.dot_general(a, b, (((1,), (1,)), ((), ())), preferred_element_type=F32)


def _dot_exact_lhs(mat01, v):
    hi, mid, lo = _split3(v)
    return _dot(mat01, hi) + _dot(mat01, mid) + _dot(mat01, lo)


def _dot_hilo(a, b_hi, b_lo):
    a_hi, a_lo = _split2(a)
    return _dot(a_hi, b_hi) + _dot(a_lo, b_hi) + _dot(a_hi, b_lo)


def _pack_bf16_pairs(v):
    c = v.shape[1] // 2
    lo = lax.bitcast_convert_type(v[:, :c].astype(BF16).astype(F32), jnp.uint32)
    hi = lax.bitcast_convert_type(v[:, c:].astype(BF16).astype(F32), jnp.uint32)
    return (lo >> 16) | (hi & jnp.uint32(0xFFFF0000))


def _unpack_bf16_pairs(w):
    lo = lax.bitcast_convert_type(w << 16, F32)
    hi = lax.bitcast_convert_type(w & jnp.uint32(0xFFFF0000), F32)
    return lo, hi


def _sigmoid(v):
    return 0.5 * jnp.tanh(0.5 * v) + 0.5


def _silu(v):
    return v * _sigmoid(v)


def _softplus(v):
    return jnp.maximum(v, 0.0) + jnp.log(1.0 + jnp.exp(-jnp.abs(v)))


def _log_sigmoid(v):
    return jnp.minimum(v, 0.0) - jnp.log(1.0 + jnp.exp(-jnp.abs(v)))


def _in_proj_kernel(offs_ref, x_ref, nw_ref, wt_ref, wsh_ref, wsl_ref, u_ref, small_ref,
                    n1_ref):
    del offs_ref
    @pl.when(pl.program_id(1) == 0)
    def _():
        x = x_ref[...]
        ms = jnp.mean(x * x, axis=-1, keepdims=True)
        n1 = x * lax.rsqrt(ms + NORM_EPS) * nw_ref[...]
        n1_ref[...] = n1.astype(BF16)
        n1_hi, n1_lo = _split2(n1)
        small_ref[...] = (_dot_nt(n1_hi, wsh_ref[...]) + _dot_nt(n1_lo, wsh_ref[...])
                          + _dot_nt(n1_hi, wsl_ref[...]))

    u_ref[...] = _dot_nt(n1_ref[...], wt_ref[...]).astype(BF16)


def _in_proj(row_offs, x2, norm_w, wt, ws_hi, ws_lo, *, tm, tn):
    m = x2.shape[0]
    grid_spec = pltpu.PrefetchScalarGridSpec(
        num_scalar_prefetch=1,
        grid=(m // tm, U_WIDTH // tn),
        in_specs=[
            pl.BlockSpec((tm, D_MODEL), lambda i, j, o: (i, 0)),
            pl.BlockSpec((1, D_MODEL), lambda i, j, o: (0, 0)),
            pl.BlockSpec((pl.Element(tn), pl.Element(D_MODEL)),
                         lambda i, j, o: (pl.multiple_of(o[j], 2 * SUBLANES), 0)),
            pl.BlockSpec((SMALL_WIDTH, D_MODEL), lambda i, j, o: (0, 0)),
            pl.BlockSpec((SMALL_WIDTH, D_MODEL), lambda i, j, o: (0, 0)),
        ],
        out_specs=[
            pl.BlockSpec((tm, tn), lambda i, j, o: (i, j)),
            pl.BlockSpec((tm, SMALL_WIDTH), lambda i, j, o: (i, 0)),
        ],
        scratch_shapes=[pltpu.VMEM((tm, D_MODEL), BF16)],
    )
    return pl.pallas_call(
        _in_proj_kernel,
        grid_spec=grid_spec,
        out_shape=[
            jax.ShapeDtypeStruct((m, U_WIDTH), BF16),
            jax.ShapeDtypeStruct((m, SMALL_WIDTH), F32),
        ],
        compiler_params=pltpu.CompilerParams(
            dimension_semantics=("arbitrary", "arbitrary"),
            vmem_limit_bytes=VMEM_LIMIT_BYTES),
        name="in_proj",
    )(row_offs, x2, norm_w, wt, ws_hi, ws_lo)


def _ssd_kernel(z_ref, x_ref, bc_ref, sm_ref, convw_ref, convb_ref, dtb_ref, alog_ref,
                dsk_ref, nw_ref, e2_ref, tri_ref, shift_ref, o_ref,
                ext_ref, xc_ref, xd_ref, xdd_ref, y_ref, state_ref):
    L = MIX_CHUNK
    G, N, P = SSD_GROUPS, SSD_STATE, SSD_HEAD_DIM
    GW = SSD_GROUP_WIDTH

    @pl.when(pl.program_id(1) == 0)
    def _():
        ext_ref[0:L, :] = jnp.zeros((L, SSD_CONV_DIM), BF16)
        state_ref[...] = jnp.zeros(state_ref.shape, F32)

    ext_ref[L:2 * L, 0:SSD_WIDTH] = x_ref[...]
    ext_ref[L:2 * L, SSD_WIDTH:SSD_CONV_DIM] = bc_ref[...]
    slab = 512
    for s in range(SSD_CONV_DIM // slab):
        cols = slice(s * slab, (s + 1) * slab)
        acc = (convb_ref[:, cols]
               + convw_ref[CONV_WIDTH - 1:CONV_WIDTH, cols] * ext_ref[L:2 * L, cols].astype(F32))
        for k in range(1, CONV_WIDTH):
            w = CONV_WIDTH - 1 - k
            acc = acc + convw_ref[w:w + 1, cols] * _dot(shift_ref[k - 1], ext_ref[:, cols])
        xc_ref[:, cols] = _silu(acc)
    ext_ref[0:L, :] = ext_ref[L:2 * L, :]

    lane = lax.broadcasted_iota(jnp.int32, (1, LANES), 1)
    a = jnp.where(lane < SSD_HEADS, -jnp.exp(alog_ref[...]), 0.0)
    dt = _softplus(sm_ref[...] + dtb_ref[...])
    la = _dot_exact_lhs(tri_ref[...], dt * a)
    la_t = la.T
    la_end = la[L - 1:L, :]

    def hilo(v):
        hi, lo = _split2(v)
        return jnp.concatenate([hi, lo], axis=1)

    def expand(v_hl, cols):
        return _dot(v_hl, e2_ref[:, cols])

    dt_hl = hilo(dt)
    ela_hl = hilo(jnp.exp(la))
    dte_hl = hilo(jnp.exp(la_end - la))
    cd_hl = hilo(jnp.broadcast_to(jnp.exp(la_end), (SUBLANES, LANES)))

    for s in range(SSD_WIDTH // slab):
        cols = slice(s * slab, (s + 1) * slab)
        xd = xc_ref[:, cols] * expand(dt_hl, cols)
        xd_ref[:, cols] = xd
        xdd_ref[:, cols] = (xd * expand(dte_hl, cols)).astype(BF16)

    row = lax.broadcasted_iota(jnp.int32, (L, L), 0)
    col = lax.broadcasted_iota(jnp.int32, (L, L), 1)
    causal = row >= col
    lane_l = lax.broadcasted_iota(jnp.int32, (L, LANES), 1)

    for g in range(G):
        bm = xc_ref[:, SSD_WIDTH + g * N:SSD_WIDTH + (g + 1) * N]
        cm = xc_ref[:, SSD_WIDTH + G * N + g * N:SSD_WIDTH + G * N + (g + 1) * N]
        cm_b = cm.astype(BF16)
        cb = _dot_nt(cm_b, bm.astype(BF16))
        gcols = slice(g * GW, (g + 1) * GW)
        y_ref[:, gcols] = (_dot(cm_b, state_ref[g].astype(BF16)) * expand(ela_hl, gcols)
                           + dsk_ref[:, gcols] * xc_ref[:, gcols])
        for p in range(GW // LANES):
            h0 = g * (GW // P) + 2 * p
            ms = []
            for h in (h0, h0 + 1):
                seg = la[:, h:h + 1] - la_t[h:h + 1, :]
                dec = jnp.where(causal, jnp.exp(jnp.minimum(seg, 0.0)), 0.0)
                ms.append((cb * dec).astype(BF16))
            pcols = slice(g * GW + p * LANES, g * GW + (p + 1) * LANES)
            xp = xd_ref[:, pcols]
            rhs = jnp.concatenate([jnp.where(lane_l < P, xp, 0.0),
                                   jnp.where(lane_l >= P, xp, 0.0)], axis=0).astype(BF16)
            y_ref[:, pcols] = y_ref[:, pcols] + _dot(jnp.concatenate(ms, axis=1), rhs)
        upd = _dot(bm.T.astype(BF16), xdd_ref[:, gcols])
        state_ref[g] = expand(cd_hl, gcols)[0:1, :] * state_ref[g] + upd

    for g in range(G):
        gcols = slice(g * GW, (g + 1) * GW)
        y = y_ref[:, gcols] * _silu(z_ref[:, gcols].astype(F32))
        ms = jnp.mean(y * y, axis=-1, keepdims=True)
        o_ref[:, gcols] = (y * lax.rsqrt(ms + NORM_EPS) * nw_ref[:, gcols]).astype(BF16)


def _ssd(u, small, conv_w, conv_b, dtb, alog, dsk_e, norm_w, e2, tri, *, batch, seq):
    L = MIX_CHUNK
    nc = seq // L
    m = batch * seq
    t_idx = jnp.arange(L)[None, :, None]
    j_idx = jnp.arange(2 * L)[None, None, :]
    k_idx = jnp.arange(1, CONV_WIDTH)[:, None, None]
    shift = (j_idx == L + t_idx - k_idx).astype(BF16)

    def rows(b, c):
        return b * nc + c

    const = lambda shape: pl.BlockSpec(shape, lambda b, c: (0,) * len(shape))
    return pl.pallas_call(
        _ssd_kernel,
        grid=(batch, nc),
        in_specs=[
            pl.BlockSpec((L, SSD_WIDTH), lambda b, c: (rows(b, c), 0)),
            pl.BlockSpec((L, SSD_WIDTH), lambda b, c: (rows(b, c), 1)),
            pl.BlockSpec((L, SSD_BC_WIDTH), lambda b, c: (rows(b, c), 8)),
            pl.BlockSpec((L, SMALL_WIDTH), lambda b, c: (rows(b, c), 0)),
            const((CONV_WIDTH, SSD_CONV_DIM)),
            const((1, SSD_CONV_DIM)),
            const((1, LANES)),
            const((1, LANES)),
            const((1, SSD_WIDTH)),
            const((1, SSD_WIDTH)),
            const((2 * LANES, SSD_WIDTH)),
            const((L, L)),
            const((CONV_WIDTH - 1, L, 2 * L)),
        ],
        out_specs=pl.BlockSpec((L, SSD_WIDTH), lambda b, c: (rows(b, c), 0)),
        out_shape=jax.ShapeDtypeStruct((m, SSD_WIDTH), BF16),
        scratch_shapes=[
            pltpu.VMEM((2 * L, SSD_CONV_DIM), BF16),
            pltpu.VMEM((L, SSD_CONV_DIM), F32),
            pltpu.VMEM((L, SSD_WIDTH), F32),
            pltpu.VMEM((L, SSD_WIDTH), BF16),
            pltpu.VMEM((L, SSD_WIDTH), F32),
            pltpu.VMEM((SSD_GROUPS, SSD_STATE, SSD_GROUP_WIDTH), F32),
        ],
        compiler_params=pltpu.CompilerParams(
            dimension_semantics=("arbitrary", "arbitrary"),
            vmem_limit_bytes=VMEM_LIMIT_BYTES),
        name="ssd",
    )(u, u, u, small, conv_w, conv_b, dtb, alog, dsk_e, norm_w, e2, tri, shift)


def _gla_kernel(q_ref, k_ref, v_ref, g_ref, sm_ref, wuh_ref, wul_ref, bup_ref, nw_ref,
                tri_ref, o_ref, cum_ref, s_ref, st_ref):
    L, C = MIX_CHUNK, GLA_SUB
    K, V = GLA_HEAD_K, GLA_HEAD_V

    @pl.when(pl.program_id(1) == 0)
    def _():
        st_ref[...] = jnp.zeros(st_ref.shape, F32)

    pre = _dot_hilo(sm_ref[...], wuh_ref[...], wul_ref[...]) + bup_ref[...]
    gk = _log_sigmoid(pre) * (1.0 / GLA_GATE_NORMALIZER)
    cum_ref[...] = _dot_exact_lhs(tri_ref[...], gk)

    row_c = lax.broadcasted_iota(jnp.int32, (C, L), 0)
    col_c = lax.broadcasted_iota(jnp.int32, (C, L), 1)
    for h in range(GLA_HEADS):
        ks = slice(h * K, (h + 1) * K)
        vs = slice(h * V, (h + 1) * V)
        cum = cum_ref[:, ks]
        cum_last = cum_ref[L - 1:L, ks]
        qh = q_ref[:, ks].astype(F32) * (K ** -0.5)
        kh = k_ref[:, ks].astype(F32)
        vh = v_ref[:, vs]
        for blk in range(L // C):
            rows = slice(blk * C, (blk + 1) * C)
            if blk == 0:
                ref_row = jnp.zeros((1, K), F32)
            else:
                ref_row = cum_ref[blk * C - 1:blk * C, ks]
            q_b = (qh[rows] * jnp.exp(cum_ref[rows, ks] - ref_row)).astype(BF16)
            n = (blk + 1) * C
            k_b = (kh[0:n] * jnp.exp(jnp.minimum(ref_row - cum[0:n], GLA_EXP_CLAMP))).astype(BF16)
            if n < L:
                k_b = jnp.concatenate([k_b, jnp.zeros((L - n, K), BF16)], axis=0)
            s_b = _dot_nt(q_b, k_b)
            s_ref[rows, :] = jnp.where(col_c <= row_c + blk * C, s_b, 0.0)
        o_intra = _dot(s_ref[...].astype(BF16), vh)
        st = st_ref[h]
        o_inter = _dot_nt((qh * jnp.exp(cum)).astype(BF16), st.astype(BF16))
        k_dec = (kh * jnp.exp(cum_last - cum)).astype(BF16)
        st_ref[h] = st * jnp.exp(cum_last) + _dot(vh.astype(F32).T.astype(BF16), k_dec)
        o = o_inter + o_intra
        ms = jnp.mean(o * o, axis=-1, keepdims=True)
        o = o * lax.rsqrt(ms + NORM_EPS) * nw_ref[...] * _silu(g_ref[:, vs].astype(F32))
        o_ref[:, vs] = o.astype(BF16)


def _gla(u, small, wup_hi, wup_lo, bup, norm_w, tri, *, batch, seq):
    L = MIX_CHUNK
    nc = seq // L
    m = batch * seq

    def rows(b, c):
        return b * nc + c

    const = lambda shape: pl.BlockSpec(shape, lambda b, c: (0,) * len(shape))
    return pl.pallas_call(
        _gla_kernel,
        grid=(batch, nc),
        in_specs=[
            pl.BlockSpec((L, GLA_KEY_WIDTH), lambda b, c: (rows(b, c), 9)),
            pl.BlockSpec((L, GLA_KEY_WIDTH), lambda b, c: (rows(b, c), 10)),
            pl.BlockSpec((L, GLA_VALUE_WIDTH), lambda b, c: (rows(b, c), 2)),
            pl.BlockSpec((L, GLA_VALUE_WIDTH), lambda b, c: (rows(b, c), 3)),
            pl.BlockSpec((L, SMALL_WIDTH), lambda b, c: (rows(b, c), 0)),
            const((SMALL_WIDTH, GLA_KEY_WIDTH)),
            const((SMALL_WIDTH, GLA_KEY_WIDTH)),
            const((1, GLA_KEY_WIDTH)),
            const((1, GLA_HEAD_V)),
            const((L, L)),
        ],
        out_specs=pl.BlockSpec((L, GLA_VALUE_WIDTH), lambda b, c: (rows(b, c), 0)),
        out_shape=jax.ShapeDtypeStruct((m, GLA_VALUE_WIDTH), BF16),
        scratch_shapes=[
            pltpu.VMEM((L, GLA_KEY_WIDTH), F32),
            pltpu.VMEM((L, L), F32),
            pltpu.VMEM((GLA_HEADS, GLA_HEAD_V, GLA_HEAD_K), F32),
        ],
        compiler_params=pltpu.CompilerParams(
            dimension_semantics=("arbitrary", "arbitrary"),
            vmem_limit_bytes=VMEM_LIMIT_BYTES),
        name="gla",
    )(u, u, u, u, small, wup_hi, wup_lo, bup, norm_w, tri)


def _out_proj_kernel(ys_ref, yg_ref, x_ref, wo1_ref, wo2_ref, nw_ref, wrh_ref, wrl_ref,
                     rb_ref, stri_ref, h1_ref, n2_ref, idx_ref, wt_ref, rank_ref, cnt_ref,
                     run_ref):
    tm = x_ref.shape[0]

    @pl.when(pl.program_id(0) == 0)
    def _():
        run_ref[...] = jnp.zeros(run_ref.shape, F32)

    h = x_ref[...] + _dot(ys_ref[...], wo1_ref[...]) + _dot(yg_ref[...], wo2_ref[...])
    h1_ref[...] = h
    ms = jnp.mean(h * h, axis=-1, keepdims=True)
    n2 = h * lax.rsqrt(ms + NORM_EPS) * nw_ref[...]
    n2_ref[...] = _pack_bf16_pairs(n2)

    lane = lax.broadcasted_iota(jnp.int32, (tm, LANES), 1)
    lane_f = lane.astype(F32)
    logits = _dot_hilo(n2, wrh_ref[...], wrl_ref[...]) + rb_ref[...]
    work = jnp.where(lane < N_EXPERTS, logits, -jnp.inf)
    vals, idxs, sels = [], [], []
    for _ in range(TOP_K):
        top = jnp.max(work, axis=-1, keepdims=True)
        idx = jnp.min(jnp.where(work == top, lane_f, float(LANES)), axis=-1, keepdims=True)
        sel = lane_f == idx
        vals.append(top)
        idxs.append(idx)
        sels.append(sel)
        work = jnp.where(sel, -jnp.inf, work)
    exps = [jnp.exp(v - vals[0]) for v in vals]
    denom = exps[0] + exps[1] + exps[2] + exps[3]

    chosen = jnp.zeros((tm, LANES), F32)
    for sel in sels:
        chosen = chosen + jnp.where(sel, 1.0, 0.0)
    before = run_ref[0:1, :] + _dot(stri_ref[...], chosen.astype(BF16))

    idx_out = jnp.zeros((tm, LANES), F32)
    wt_out = jnp.zeros((tm, LANES), F32)
    rank_out = jnp.zeros((tm, LANES), F32)
    for j in range(TOP_K):
        rank_j = jnp.sum(jnp.where(sels[j], before, 0.0), axis=-1, keepdims=True)
        idx_out = jnp.where(lane == j, idxs[j], idx_out)
        wt_out = jnp.where(lane == j, exps[j] / denom, wt_out)
        rank_out = jnp.where(lane == j, rank_j, rank_out)
    idx_ref[...] = idx_out.astype(jnp.int32)
    wt_ref[...] = wt_out
    rank_ref[...] = rank_out.astype(jnp.int32)

    run_ref[...] = run_ref[...] + jnp.sum(chosen, axis=0, keepdims=True)
    cnt_ref[...] = run_ref[...].astype(jnp.int32)


def _out_proj(y_ssd, y_gla, x2, wo1, wo2, norm_w, wr_hi, wr_lo, rb, stri, *, tm):
    m = x2.shape[0]
    const = lambda shape: pl.BlockSpec(shape, lambda i: (0,) * len(shape))
    tile = lambda w: pl.BlockSpec((tm, w), lambda i: (i, 0))
    return pl.pallas_call(
        _out_proj_kernel,
        grid=(m // tm,),
        in_specs=[
            tile(SSD_WIDTH), tile(GLA_VALUE_WIDTH), tile(D_MODEL),
            const((SSD_WIDTH, D_MODEL)), const((GLA_VALUE_WIDTH, D_MODEL)),
            const((1, D_MODEL)),
            const((D_MODEL, LANES)), const((D_MODEL, LANES)), const((1, LANES)),
            const((tm, tm)),
        ],
        out_specs=[
            tile(D_MODEL), tile(PACKED_WIDTH), tile(LANES), tile(LANES), tile(LANES),
            const((SUBLANES, LANES)),
        ],
        out_shape=[
            jax.ShapeDtypeStruct((m, D_MODEL), F32),
            jax.ShapeDtypeStruct((m, PACKED_WIDTH), jnp.uint32),
            jax.ShapeDtypeStruct((m, LANES), jnp.int32),
            jax.ShapeDtypeStruct((m, LANES), F32),
            jax.ShapeDtypeStruct((m, LANES), jnp.int32),
            jax.ShapeDtypeStruct((SUBLANES, LANES), jnp.int32),
        ],
        scratch_shapes=[pltpu.VMEM((SUBLANES, LANES), F32)],
        compiler_params=pltpu.CompilerParams(
            dimension_semantics=("arbitrary",),
            vmem_limit_bytes=VMEM_LIMIT_BYTES),
        name="out_proj",
    )(y_ssd, y_gla, x2, wo1, wo2, norm_w, wr_hi, wr_lo, rb, stri)


def _dispatch_kernel(fill_ref, dest_ref, n2_ref, xg_hbm, zero_ref, sem, zsem):
    tm = TOKEN_TILE

    @pl.when(pl.program_id(0) == 0)
    def _():
        zero_ref[...] = jnp.zeros(zero_ref.shape, zero_ref.dtype)

        def fill(b):
            row = pl.multiple_of(b * ROW_BLOCK, ROW_BLOCK)
            return pltpu.make_async_copy(zero_ref, xg_hbm.at[pl.ds(row, ROW_BLOCK)], zsem)

        def start_fill(b, carry):
            @pl.when(fill_ref[b] > 0)
            def _():
                fill(b).start()
            return carry

        def wait_fill(b, carry):
            @pl.when(fill_ref[b] > 0)
            def _():
                fill(b).wait()
            return carry

        lax.fori_loop(0, fill_ref.shape[0], start_fill, 0)
        lax.fori_loop(0, fill_ref.shape[0], wait_fill, 0)

    def copy(r, row):
        return pltpu.make_async_copy(n2_ref.at[pl.ds(r, 1)], xg_hbm.at[pl.ds(row, 1)], sem)

    for r in range(tm):
        for j in range(TOP_K):
            copy(r, dest_ref[0, 0, TOP_K * r + j]).start(priority=j % 2)
    for r in range(tm):
        for j in range(TOP_K):
            copy(r, 0).wait()


def _dispatch(fill, dest3, n2p):
    n_tiles = dest3.shape[0]
    n_rows = fill.shape[0] * ROW_BLOCK
    grid_spec = pltpu.PrefetchScalarGridSpec(
        num_scalar_prefetch=1,
        grid=(n_tiles,),
        in_specs=[
            pl.BlockSpec((1, 1, TOP_K * TOKEN_TILE), lambda i, fl: (i, 0, 0),
                         memory_space=pltpu.SMEM),
            pl.BlockSpec((TOKEN_TILE, PACKED_WIDTH), lambda i, fl: (i, 0)),
        ],
        out_specs=pl.BlockSpec(memory_space=pl.ANY),
        scratch_shapes=[
            pltpu.VMEM((ROW_BLOCK, PACKED_WIDTH), jnp.uint32),
            pltpu.SemaphoreType.DMA(()),
            pltpu.SemaphoreType.DMA(()),
        ],
    )
    return pl.pallas_call(
        _dispatch_kernel,
        grid_spec=grid_spec,
        out_shape=jax.ShapeDtypeStruct((n_rows, PACKED_WIDTH), jnp.uint32),
        compiler_params=pltpu.CompilerParams(
            dimension_semantics=("arbitrary",), has_side_effects=True),
        name="dispatch",
    )(fill, dest3, n2p)


def _ffn_kernel(sbe_ref, nact_ref, x_ref, wg_ref, wu_ref, wd_ref, bg_ref, bu_ref, bd_ref,
                o_ref, xb_ref, act_ref, wgb_ref, wub_ref, wdb_ref):
    del sbe_ref
    i = pl.program_id(0)
    f = pl.program_id(1)
    nf = D_FF // FFN_TILE
    nact = nact_ref[i]

    full = nact == SUB_PER_SUPER

    def gate_up(sb):
        rows = slice(sb * ROW_BLOCK, (sb + 1) * ROW_BLOCK)
        xs = xb_ref[rows, :]
        gate = jnp.minimum(_dot(xs, wgb_ref[...]) + bg_ref[0], SWIGLU_LIMIT)
        up = jnp.clip(_dot(xs, wub_ref[...]) + bu_ref[0], -SWIGLU_LIMIT, SWIGLU_LIMIT)
        act_ref[f, rows, :] = ((up + 1.0) * gate * _sigmoid(SWIGLU_ALPHA * gate)).astype(BF16)

    def down(sb):
        rows = slice(sb * ROW_BLOCK, (sb + 1) * ROW_BLOCK)
        act = jnp.concatenate([act_ref[k, rows, :] for k in range(nf)], axis=1)
        out = _dot(act, wdb_ref[...]) + bd_ref[0]
        o_ref[rows, :] = _pack_bf16_pairs(out)

    @pl.when(jnp.logical_and(nact > 0, f == 0))
    def _():
        for sb in range(SUB_PER_SUPER):
            rows = slice(sb * ROW_BLOCK, (sb + 1) * ROW_BLOCK)
            lo, hi = _unpack_bf16_pairs(x_ref[rows, :])
            xb_ref[rows, 0:PACKED_WIDTH] = lo.astype(BF16)
            xb_ref[rows, PACKED_WIDTH:D_MODEL] = hi.astype(BF16)

    def cast_weights():
        wgb_ref[...] = wg_ref[0].astype(BF16)
        wub_ref[...] = wu_ref[0].astype(BF16)
        wdb_ref[pl.ds(pl.multiple_of(f * FFN_TILE, FFN_TILE), FFN_TILE), :] = (
            wd_ref[0].astype(BF16))

    @pl.when(full)
    def _():
        cast_weights()
        for sb in range(SUB_PER_SUPER):
            gate_up(sb)

    @pl.when(jnp.logical_and(nact > 0, jnp.logical_not(full)))
    def _():
        cast_weights()
        gate_up(0)

    for sb in range(1, SUB_PER_SUPER - 1):
        @pl.when(jnp.logical_and(sb < nact, jnp.logical_not(full)))
        def _():
            gate_up(sb)

    @pl.when(f == nf - 1)
    def _():
        @pl.when(full)
        def _():
            for sb in range(SUB_PER_SUPER):
                down(sb)

        for sb in range(SUB_PER_SUPER):
            rows = slice(sb * ROW_BLOCK, (sb + 1) * ROW_BLOCK)

            @pl.when(jnp.logical_and(sb < nact, jnp.logical_not(full)))
            def _():
                down(sb)

            @pl.when(sb >= nact)
            def _():
                o_ref[rows, :] = jnp.zeros((ROW_BLOCK, PACKED_WIDTH), jnp.uint32)


def _ffn(sb_expert, sb_nact, xg, wgu, wd, bgu3, bd3):
    n_super = sb_expert.shape[0]
    nf = D_FF // FFN_TILE
    grid_spec = pltpu.PrefetchScalarGridSpec(
        num_scalar_prefetch=2,
        grid=(n_super, nf),
        in_specs=[
            pl.BlockSpec((SUPER_ROWS, PACKED_WIDTH), lambda i, f, e, n: (i, 0)),
            pl.BlockSpec((1, D_MODEL, FFN_TILE), lambda i, f, e, n: (e[i], 0, f)),
            pl.BlockSpec((1, D_MODEL, FFN_TILE), lambda i, f, e, n: (e[i], 0, f + nf)),
            pl.BlockSpec((1, FFN_TILE, D_MODEL), lambda i, f, e, n: (e[i], f, 0)),
            pl.BlockSpec((1, 1, FFN_TILE), lambda i, f, e, n: (e[i], 0, f)),
            pl.BlockSpec((1, 1, FFN_TILE), lambda i, f, e, n: (e[i], 0, f + nf)),
            pl.BlockSpec((1, 1, D_MODEL), lambda i, f, e, n: (e[i], 0, 0)),
        ],
        out_specs=pl.BlockSpec((SUPER_ROWS, PACKED_WIDTH), lambda i, f, e, n: (i, 0)),
        scratch_shapes=[
            pltpu.VMEM((SUPER_ROWS, D_MODEL), BF16),
            pltpu.VMEM((D_FF // FFN_TILE, SUPER_ROWS, FFN_TILE), BF16),
            pltpu.VMEM((D_MODEL, FFN_TILE), BF16),
            pltpu.VMEM((D_MODEL, FFN_TILE), BF16),
            pltpu.VMEM((D_FF, D_MODEL), BF16),
        ],
    )
    return pl.pallas_call(
        _ffn_kernel,
        grid_spec=grid_spec,
        out_shape=jax.ShapeDtypeStruct(xg.shape, jnp.uint32),
        compiler_params=pltpu.CompilerParams(
            dimension_semantics=("arbitrary", "arbitrary"),
            vmem_limit_bytes=VMEM_LIMIT_BYTES),
        name="ffn",
    )(sb_expert, sb_nact, xg, wgu, wgu, wd, bgu3, bgu3, bd3)


def _combine_kernel(dcur_ref, dnxt_ref, wt_ref, h1_ref, fw_ref, eo_hbm, o_ref, buf, sem):
    tm = TOKEN_TILE
    i = pl.program_id(0)
    n = pl.num_programs(0)
    slot = lax.rem(i, 2)

    def copy(row, s, r, j):
        return pltpu.make_async_copy(
            eo_hbm.at[pl.ds(row, 1)], buf.at[s, j, pl.ds(r, 1)], sem.at[s])

    def start_all(dref, s):
        for r in range(tm):
            for j in range(TOP_K):
                copy(dref[0, 0, TOP_K * r + j], s, r, j).start(priority=j % 2)

    def wait_all(s):
        for r in range(tm):
            for j in range(TOP_K):
                copy(0, s, r, j).wait()

    @pl.when(i == 0)
    def _():
        start_all(dcur_ref, 0)

    for s in range(2):
        @pl.when(jnp.logical_and(i + 1 < n, slot != s))
        def _():
            start_all(dnxt_ref, s)

    for s in range(2):
        @pl.when(slot == s)
        def _():
            wait_all(s)

    wt = wt_ref[...]
    half = PACKED_WIDTH
    y_lo = h1_ref[:, 0:half]
    y_hi = h1_ref[:, half:D_MODEL]
    for j in range(TOP_K):
        lo, hi = _unpack_bf16_pairs(buf[slot, j])
        y_lo = y_lo + lo * wt[:, j:j + 1]
        y_hi = y_hi + hi * wt[:, j:j + 1]
    ssq = (jnp.sum(y_lo * y_lo, axis=-1, keepdims=True)
           + jnp.sum(y_hi * y_hi, axis=-1, keepdims=True))
    scale = lax.rsqrt(ssq * (1.0 / D_MODEL) + NORM_EPS)
    o_ref[:, 0:half] = y_lo * scale * fw_ref[:, 0:half]
    o_ref[:, half:D_MODEL] = y_hi * scale * fw_ref[:, half:D_MODEL]


def _combine(dest3, wt, h1, final_w, eo):
    n_tiles = dest3.shape[0]
    tm = TOKEN_TILE
    m = h1.shape[0]
    smem = lambda fn: pl.BlockSpec((1, 1, TOP_K * tm), fn, memory_space=pltpu.SMEM)
    return pl.pallas_call(
        _combine_kernel,
        grid=(n_tiles,),
        in_specs=[
            smem(lambda i: (i, 0, 0)),
            smem(lambda i: (jnp.minimum(i + 1, n_tiles - 1), 0, 0)),
            pl.BlockSpec((tm, LANES), lambda i: (i, 0)),
            pl.BlockSpec((tm, D_MODEL), lambda i: (i, 0)),
            pl.BlockSpec((1, D_MODEL), lambda i: (0, 0)),
            pl.BlockSpec(memory_space=pl.ANY),
        ],
        out_specs=pl.BlockSpec((tm, D_MODEL), lambda i: (i, 0)),
        out_shape=jax.ShapeDtypeStruct((m, D_MODEL), F32),
        scratch_shapes=[
            pltpu.VMEM((2, TOP_K, tm, PACKED_WIDTH), jnp.uint32),
            pltpu.SemaphoreType.DMA((2,)),
        ],
        compiler_params=pltpu.CompilerParams(
            dimension_semantics=("arbitrary",),
            vmem_limit_bytes=VMEM_LIMIT_BYTES),
        name="combine",
    )(dest3, dest3, wt, h1, final_w, eo)


def _pad_lanes(v, width=LANES):
    return jnp.pad(v, [(0, 0)] * (v.ndim - 1) + [(0, width - v.shape[-1])])


def _layer(h, norm_mix_w, in_proj_w, conv_w, conv_b, dt_bias, a_log, d_skip, ssd_norm_w,
           gate_up_w, gate_up_b, gla_norm_w, out_proj_w, norm_ffn_w, router_w, router_b,
           w_gate_up, b_gate_up, w_down, b_down, out_norm_w):
    batch, seq, _ = h.shape
    m = batch * seq
    x2 = h.reshape(m, D_MODEL)
    L = MIX_CHUNK

    w = in_proj_w.T
    o_xbc = SSD_WIDTH
    o_dt = o_xbc + SSD_CONV_DIM
    o_q = o_dt + SSD_HEADS
    o_k = o_q + GLA_KEY_WIDTH
    o_v = o_k + GLA_KEY_WIDTH
    o_g = o_v + GLA_VALUE_WIDTH
    o_gl = o_g + GLA_VALUE_WIDTH
    tn = 1024
    seg_starts = (0, o_xbc, o_v, o_g, o_xbc + SSD_WIDTH, o_q, o_k)
    seg_widths = (SSD_WIDTH, SSD_WIDTH, GLA_VALUE_WIDTH, GLA_VALUE_WIDTH, SSD_BC_WIDTH,
                  GLA_KEY_WIDTH, GLA_KEY_WIDTH)
    row_offs = jnp.array([s + t for s, wd_ in zip(seg_starts, seg_widths)
                          for t in range(0, wd_, tn)], jnp.int32)
    n_small = SSD_HEADS + GLA_GATE_RANK
    wt_small = jnp.concatenate([w[o_dt:o_q], w[o_gl:o_gl + GLA_GATE_RANK],
                                jnp.zeros((SMALL_WIDTH - n_small, D_MODEL), F32)], axis=0)
    ws_hi, ws_lo = _split2(wt_small)

    u, small = _in_proj(row_offs, x2, norm_mix_w.reshape(1, D_MODEL), w.astype(BF16),
                        ws_hi, ws_lo, tm=min(1024, m), tn=tn)

    tri = jnp.tril(jnp.ones((L, L), F32)).astype(BF16)
    head_of_col = jnp.arange(SSD_WIDTH) // SSD_HEAD_DIM
    expand = (jnp.arange(LANES)[:, None] == head_of_col[None, :]).astype(BF16)
    e2 = jnp.concatenate([expand, expand], axis=0)
    y_ssd = _ssd(u, small, conv_w, conv_b.reshape(1, -1),
                 _pad_lanes(dt_bias.reshape(1, -1)), _pad_lanes(a_log.reshape(1, -1)),
                 jnp.repeat(d_skip, SSD_HEAD_DIM).reshape(1, -1), ssd_norm_w.reshape(1, -1),
                 e2, tri, batch=batch, seq=seq)

    wup = jnp.zeros((SMALL_WIDTH, GLA_KEY_WIDTH), F32)
    wup = wup.at[SMALL_GATE_OFF:SMALL_GATE_OFF + GLA_GATE_RANK].set(gate_up_w)
    wup_hi, wup_lo = _split2(wup)
    y_gla = _gla(u, small, wup_hi, wup_lo, gate_up_b.reshape(1, -1), gla_norm_w.reshape(1, -1),
                 tri, batch=batch, seq=seq)

    tm = TOKEN_TILE
    wr_hi, wr_lo = _split2(_pad_lanes(router_w))
    stri = jnp.tril(jnp.ones((tm, tm), F32), k=-1).astype(BF16)
    wo = out_proj_w.astype(BF16)
    h1, n2, top_idx, top_w, rank, cnt = _out_proj(
        y_ssd, y_gla, x2, wo[:SSD_WIDTH], wo[SSD_WIDTH:], norm_ffn_w.reshape(1, -1),
        wr_hi, wr_lo, _pad_lanes(router_b.reshape(1, -1)), stri, tm=tm)

    counts = cnt[0, :N_EXPERTS]
    n_sb = (counts + SUPER_ROWS - 1) // SUPER_ROWS
    sb_end = jnp.cumsum(n_sb)
    sb_start = sb_end - n_sb
    dest = (sb_start * SUPER_ROWS)[top_idx[:, :TOP_K]] + rank[:, :TOP_K]
    dest3 = dest.astype(jnp.int32).reshape(m // tm, 1, TOP_K * tm)

    n_super = (m * TOP_K + N_EXPERTS * (SUPER_ROWS - 1)) // SUPER_ROWS
    sb_ids = jnp.arange(n_super)
    total_sb = sb_end[-1]
    sb_live = jnp.minimum(sb_ids, total_sb - 1)
    sb_expert = jnp.minimum(jnp.searchsorted(sb_end, sb_live, side="right"), N_EXPERTS - 1)
    rows_left = counts[sb_expert] - (sb_live - sb_start[sb_expert]) * SUPER_ROWS
    sb_nact = jnp.where(sb_ids < total_sb,
                        (jnp.clip(rows_left, 0, SUPER_ROWS) + ROW_BLOCK - 1) // ROW_BLOCK, 0)

    sb_rows = jnp.where(sb_ids < total_sb, jnp.clip(rows_left, 0, SUPER_ROWS), 0)
    sub_rows = sb_rows[:, None] - jnp.arange(SUB_PER_SUPER)[None, :] * ROW_BLOCK
    fill = (sub_rows < ROW_BLOCK).astype(jnp.int32).reshape(-1)
    xg = _dispatch(fill, dest3, n2)
    eo = _ffn(sb_expert.astype(jnp.int32), sb_nact.astype(jnp.int32), xg,
              w_gate_up, w_down,
              b_gate_up.reshape(N_EXPERTS, 1, 2 * D_FF), b_down.reshape(N_EXPERTS, 1, D_MODEL))
    out = _combine(dest3, top_w, h1, out_norm_w.reshape(1, -1), eo)
    return out.reshape(batch, seq, D_MODEL)


def kernel(x, norm_mix_w, in_proj_w, conv_w, conv_b, dt_bias, a_log, d_skip, ssd_norm_w,
           gate_up_w, gate_up_b, gla_norm_w, out_proj_w, norm_ffn_w, router_w, router_b,
           w_gate_up, b_gate_up, w_down, b_down, final_norm_w):
    assert norm_mix_w.shape[0] == 1
    return _layer(x, norm_mix_w[0], in_proj_w[0], conv_w[0], conv_b[0], dt_bias[0], a_log[0],
                  d_skip[0], ssd_norm_w[0], gate_up_w[0], gate_up_b[0], gla_norm_w[0],
                  out_proj_w[0], norm_ffn_w[0], router_w[0], router_b[0], w_gate_up[0],
                  b_gate_up[0], w_down[0], b_down[0], final_norm_w)
```

```python
import functools

import jax
import jax.numpy as jnp
from jax import lax
from jax.experimental import pallas as pl
from jax.experimental.pallas import tpu as pltpu

F32 = jnp.float32
BF16 = jnp.bfloat16

D_MODEL = 2048
NORM_EPS = 1e-5

SSD_WIDTH = 2048
SSD_HEAD_DIM = 64
SSD_HEADS = 32
SSD_GROUPS = 4
SSD_STATE = 128
CONV_WIDTH = 4
SSD_GROUP_WIDTH = SSD_WIDTH // SSD_GROUPS
SSD_BC_WIDTH = 2 * SSD_GROUPS * SSD_STATE
SSD_CONV_DIM = SSD_WIDTH + SSD_BC_WIDTH

GLA_HEADS = 4
GLA_KEY_WIDTH = 1024
GLA_VALUE_WIDTH = 2048
GLA_HEAD_K = 256
GLA_HEAD_V = 512
GLA_GATE_RANK = 16
GLA_GATE_NORMALIZER = 16.0

N_EXPERTS = 32
TOP_K = 4
D_FF = 2048
SWIGLU_LIMIT = 7.0
SWIGLU_ALPHA = 1.702

LANES = 128
SUBLANES = 8
VMEM_LIMIT_BYTES = 56 * 1024 * 1024

U_WIDTH = 11264
SMALL_WIDTH = LANES
SMALL_DT_OFF = 0
SMALL_GATE_OFF = SSD_HEADS

MIX_CHUNK = 128
GLA_SUB = 16
GLA_EXP_CLAMP = 80.0

ROW_BLOCK = 256
SUPER_ROWS = 768
SUB_PER_SUPER = SUPER_ROWS // ROW_BLOCK
PACKED_WIDTH = D_MODEL // 2
FFN_TILE = 512
TOKEN_TILE = 256


def _split2(v):
    hi = v.astype(BF16)
    lo = (v - hi.astype(F32)).astype(BF16)
    return hi, lo


def _split3(v):
    hi = v.astype(BF16)
    r = v - hi.astype(F32)
    mid = r.astype(BF16)
    lo = (r - mid.astype(F32)).astype(BF16)
    return hi, mid, lo


def _dot(a, b):
    return jnp.dot(a, b, preferred_element_type=F32)


def _dot_nt(a, b):
    return lax.dot_general(a, b, (((1,), (1,)), ((), ())), preferred_element_type=F32)


def _dot_exact_lhs(mat01, v):
    hi, mid, lo = _split3(v)
    return _dot(mat01, hi) + _dot(mat01, mid) + _dot(mat01, lo)


def _dot_hilo(a, b_hi, b_lo):
    a_hi, a_lo = _split2(a)
    return _dot(a_hi, b_hi) + _dot(a_lo, b_hi) + _dot(a_hi, b_lo)


def _pack_bf16_pairs(v):
    c = v.shape[1] // 2
    lo = lax.bitcast_convert_type(v[:, :c].astype(BF16).astype(F32), jnp.uint32)
    hi = lax.bitcast_convert_type(v[:, c:].astype(BF16).astype(F32), jnp.uint32)
    return (lo >> 16) | (hi & jnp.uint32(0xFFFF0000))


def _unpack_bf16_pairs(w):
    lo = lax.bitcast_convert_type(w << 16, F32)
    hi = lax.bitcast_convert_type(w & jnp.uint32(0xFFFF0000), F32)
    return lo, hi


def _sigmoid(v):
    return 0.5 * jnp.tanh(0.5 * v) + 0.5


def _silu(v):
    return v * _sigmoid(v)


def _softplus(v):
    return jnp.maximum(v, 0.0) + jnp.log(1.0 + jnp.exp(-jnp.abs(v)))


def _log_sigmoid(v):
    return jnp.minimum(v, 0.0) - jnp.log(1.0 + jnp.exp(-jnp.abs(v)))


def _in_proj_kernel(offs_ref, x_ref, nw_ref, wt_ref, wsh_ref, wsl_ref, u_ref, small_ref,
                    n1_ref):
    del offs_ref
    @pl.when(pl.program_id(1) == 0)
    def _():
        x = x_ref[...]
        ms = jnp.mean(x * x, axis=-1, keepdims=True)
        n1 = x * lax.rsqrt(ms + NORM_EPS) * nw_ref[...]
        n1_ref[...] = n1.astype(BF16)
        n1_hi, n1_lo = _split2(n1)
        small_ref[...] = (_dot_nt(n1_hi, wsh_ref[...]) + _dot_nt(n1_lo, wsh_ref[...])
                          + _dot_nt(n1_hi, wsl_ref[...]))

    u_ref[...] = _dot_nt(n1_ref[...], wt_ref[...]).astype(BF16)


def _in_proj(row_offs, x2, norm_w, wt, ws_hi, ws_lo, *, tm, tn):
    m = x2.shape[0]
    grid_spec = pltpu.PrefetchScalarGridSpec(
        num_scalar_prefetch=1,
        grid=(m // tm, U_WIDTH // tn),
        in_specs=[
            pl.BlockSpec((tm, D_MODEL), lambda i, j, o: (i, 0)),
            pl.BlockSpec((1, D_MODEL), lambda i, j, o: (0, 0)),
            pl.BlockSpec((pl.Element(tn), pl.Element(D_MODEL)),
                         lambda i, j, o: (pl.multiple_of(o[j], 2 * SUBLANES), 0)),
            pl.BlockSpec((SMALL_WIDTH, D_MODEL), lambda i, j, o: (0, 0)),
            pl.BlockSpec((SMALL_WIDTH, D_MODEL), lambda i, j, o: (0, 0)),
        ],
        out_specs=[
            pl.BlockSpec((tm, tn), lambda i, j, o: (i, j)),
            pl.BlockSpec((tm, SMALL_WIDTH), lambda i, j, o: (i, 0)),
        ],
        scratch_shapes=[pltpu.VMEM((tm, D_MODEL), BF16)],
    )
    return pl.pallas_call(
        _in_proj_kernel,
        grid_spec=grid_spec,
        out_shape=[
            jax.ShapeDtypeStruct((m, U_WIDTH), BF16),
            jax.ShapeDtypeStruct((m, SMALL_WIDTH), F32),
        ],
        compiler_params=pltpu.CompilerParams(
            dimension_semantics=("arbitrary", "arbitrary"),
            vmem_limit_bytes=VMEM_LIMIT_BYTES),
        name="in_proj",
    )(row_offs, x2, norm_w, wt, ws_hi, ws_lo)


def _ssd_kernel(z_ref, x_ref, bc_ref, sm_ref, convw_ref, convb_ref, dtb_ref, alog_ref,
                dsk_ref, nw_ref, e2_ref, tri_ref, shift_ref, o_ref,
                ext_ref, xc_ref, xd_ref, xdd_ref, y_ref, state_ref):
    L = MIX_CHUNK
    G, N, P = SSD_GROUPS, SSD_STATE, SSD_HEAD_DIM
    GW = SSD_GROUP_WIDTH

    @pl.when(pl.program_id(1) == 0)
    def _():
        ext_ref[0:L, :] = jnp.zeros((L, SSD_CONV_DIM), BF16)
        state_ref[...] = jnp.zeros(state_ref.shape, F32)

    ext_ref[L:2 * L, 0:SSD_WIDTH] = x_ref[...]
    ext_ref[L:2 * L, SSD_WIDTH:SSD_CONV_DIM] = bc_ref[...]
    slab = 512
    for s in range(SSD_CONV_DIM // slab):
        cols = slice(s * slab, (s + 1) * slab)
        acc = (convb_ref[:, cols]
               + convw_ref[CONV_WIDTH - 1:CONV_WIDTH, cols] * ext_ref[L:2 * L, cols].astype(F32))
        for k in range(1, CONV_WIDTH):
            w = CONV_WIDTH - 1 - k
            acc = acc + convw_ref[w:w + 1, cols] * _dot(shift_ref[k - 1], ext_ref[:, cols])
        xc_ref[:, cols] = _silu(acc)
    ext_ref[0:L, :] = ext_ref[L:2 * L, :]

    lane = lax.broadcasted_iota(jnp.int32, (1, LANES), 1)
    a = jnp.where(lane < SSD_HEADS, -jnp.exp(alog_ref[...]), 0.0)
    dt = _softplus(sm_ref[...] + dtb_ref[...])
    la = _dot_exact_lhs(tri_ref[...], dt * a)
    la_t = la.T
    la_end = la[L - 1:L, :]

    def hilo(v):
        hi, lo = _split2(v)
        return jnp.concatenate([hi, lo], axis=1)

    def expand(v_hl, cols):
        return _dot(v_hl, e2_ref[:, cols])

    dt_hl = hilo(dt)
    ela_hl = hilo(jnp.exp(la))
    dte_hl = hilo(jnp.exp(la_end - la))
    cd_hl = hilo(jnp.broadcast_to(jnp.exp(la_end), (SUBLANES, LANES)))

    for s in range(SSD_WIDTH // slab):
        cols = slice(s * slab, (s + 1) * slab)
        xd = xc_ref[:, cols] * expand(dt_hl, cols)
        xd_ref[:, cols] = xd
        xdd_ref[:, cols] = (xd * expand(dte_hl, cols)).astype(BF16)

    row = lax.broadcasted_iota(jnp.int32, (L, L), 0)
    col = lax.broadcasted_iota(jnp.int32, (L, L), 1)
    causal = row >= col
    lane_l = lax.broadcasted_iota(jnp.int32, (L, LANES), 1)

    for g in range(G):
        bm = xc_ref[:, SSD_WIDTH + g * N:SSD_WIDTH + (g + 1) * N]
        cm = xc_ref[:, SSD_WIDTH + G * N + g * N:SSD_WIDTH + G * N + (g + 1) * N]
        cm_b = cm.astype(BF16)
        cb = _dot_nt(cm_b, bm.astype(BF16))
        gcols = slice(g * GW, (g + 1) * GW)
        y_ref[:, gcols] = (_dot(cm_b, state_ref[g].astype(BF16)) * expand(ela_hl, gcols)
                           + dsk_ref[:, gcols] * xc_ref[:, gcols])
        for p in range(GW // LANES):
            h0 = g * (GW // P) + 2 * p
            ms = []
            for h in (h0, h0 + 1):
                seg = la[:, h:h + 1] - la_t[h:h + 1, :]
                dec = jnp.where(causal, jnp.exp(jnp.minimum(seg, 0.0)), 0.0)
                ms.append((cb * dec).astype(BF16))
            pcols = slice(g * GW + p * LANES, g * GW + (p + 1) * LANES)
            xp = xd_ref[:, pcols]
            rhs = jnp.concatenate([jnp.where(lane_l < P, xp, 0.0),
                                   jnp.where(lane_l >= P, xp, 0.0)], axis=0).astype(BF16)
            y_ref[:, pcols] = y_ref[:, pcols] + _dot(jnp.concatenate(ms, axis=1), rhs)
        upd = _dot(bm.T.astype(BF16), xdd_ref[:, gcols])
        state_ref[g] = expand(cd_hl, gcols)[0:1, :] * state_ref[g] + upd

    for g in range(G):
        gcols = slice(g * GW, (g + 1) * GW)
        y = y_ref[:, gcols] * _silu(z_ref[:, gcols].astype(F32))
        ms = jnp.mean(y * y, axis=-1, keepdims=True)
        o_ref[:, gcols] = (y * lax.rsqrt(ms + NORM_EPS) * nw_ref[:, gcols]).astype(BF16)


def _ssd(u, small, conv_w, conv_b, dtb, alog, dsk_e, norm_w, e2, tri, *, batch, seq):
    L = MIX_CHUNK
    nc = seq // L
    m = batch * seq
    t_idx = jnp.arange(L)[None, :, None]
    j_idx = jnp.arange(2 * L)[None, None, :]
    k_idx = jnp.arange(1, CONV_WIDTH)[:, None, None]
    shift = (j_idx == L + t_idx - k_idx).astype(BF16)

    def rows(b, c):
        return b * nc + c

    const = lambda shape: pl.BlockSpec(shape, lambda b, c: (0,) * len(shape))
    return pl.pallas_call(
        _ssd_kernel,
        grid=(batch, nc),
        in_specs=[
            pl.BlockSpec((L, SSD_WIDTH), lambda b, c: (rows(b, c), 0)),
            pl.BlockSpec((L, SSD_WIDTH), lambda b, c: (rows(b, c), 1)),
            pl.BlockSpec((L, SSD_BC_WIDTH), lambda b, c: (rows(b, c), 8)),
            pl.BlockSpec((L, SMALL_WIDTH), lambda b, c: (rows(b, c), 0)),
            const((CONV_WIDTH, SSD_CONV_DIM)),
            const((1, SSD_CONV_DIM)),
            const((1, LANES)),
            const((1, LANES)),
            const((1, SSD_WIDTH)),
            const((1, SSD_WIDTH)),
            const((2 * LANES, SSD_WIDTH)),
            const((L, L)),
            const((CONV_WIDTH - 1, L, 2 * L)),
        ],
        out_specs=pl.BlockSpec((L, SSD_WIDTH), lambda b, c: (rows(b, c), 0)),
        out_shape=jax.ShapeDtypeStruct((m, SSD_WIDTH), BF16),
        scratch_shapes=[
            pltpu.VMEM((2 * L, SSD_CONV_DIM), BF16),
            pltpu.VMEM((L, SSD_CONV_DIM), F32),
            pltpu.VMEM((L, SSD_WIDTH), F32),
            pltpu.VMEM((L, SSD_WIDTH), BF16),
            pltpu.VMEM((L, SSD_WIDTH), F32),
            pltpu.VMEM((SSD_GROUPS, SSD_STATE, SSD_GROUP_WIDTH), F32),
        ],
        compiler_params=pltpu.CompilerParams(
            dimension_semantics=("arbitrary", "arbitrary"),
            vmem_limit_bytes=VMEM_LIMIT_BYTES),
        name="ssd",
    )(u, u, u, small, conv_w, conv_b, dtb, alog, dsk_e, norm_w, e2, tri, shift)


def _gla_kernel(q_ref, k_ref, v_ref, g_ref, sm_ref, wuh_ref, wul_ref, bup_ref, nw_ref,
                tri_ref, o_ref, cum_ref, s_ref, st_ref):
    L, C = MIX_CHUNK, GLA_SUB
    K, V = GLA_HEAD_K, GLA_HEAD_V

    @pl.when(pl.program_id(1) == 0)
    def _():
        st_ref[...] = jnp.zeros(st_ref.shape, F32)

    pre = _dot_hilo(sm_ref[...], wuh_ref[...], wul_ref[...]) + bup_ref[...]
    gk = _log_sigmoid(pre) * (1.0 / GLA_GATE_NORMALIZER)
    cum_ref[...] = _dot_exact_lhs(tri_ref[...], gk)

    row_c = lax.broadcasted_iota(jnp.int32, (C, L), 0)
    col_c = lax.broadcasted_iota(jnp.int32, (C, L), 1)
    for h in range(GLA_HEADS):
        ks = slice(h * K, (h + 1) * K)
        vs = slice(h * V, (h + 1) * V)
        cum = cum_ref[:, ks]
        cum_last = cum_ref[L - 1:L, ks]
        qh = q_ref[:, ks].astype(F32) * (K ** -0.5)
        kh = k_ref[:, ks].astype(F32)
        vh = v_ref[:, vs]
        for blk in range(L // C):
            rows = slice(blk * C, (blk + 1) * C)
            if blk == 0:
                ref_row = jnp.zeros((1, K), F32)
            else:
                ref_row = cum_ref[blk * C - 1:blk * C, ks]
            q_b = (qh[rows] * jnp.exp(cum_ref[rows, ks] - ref_row)).astype(BF16)
            n = (blk + 1) * C
            k_b = (kh[0:n] * jnp.exp(jnp.minimum(ref_row - cum[0:n], GLA_EXP_CLAMP))).astype(BF16)
            if n < L:
                k_b = jnp.concatenate([k_b, jnp.zeros((L - n, K), BF16)], axis=0)
            s_b = _dot_nt(q_b, k_b)
            s_ref[rows, :] = jnp.where(col_c <= row_c + blk * C, s_b, 0.0)
        o_intra = _dot(s_ref[...].astype(BF16), vh)
        st = st_ref[h]
        o_inter = _dot_nt((qh * jnp.exp(cum)).astype(BF16), st.astype(BF16))
        k_dec = (kh * jnp.exp(cum_last - cum)).astype(BF16)
        st_ref[h] = st * jnp.exp(cum_last) + _dot(vh.astype(F32).T.astype(BF16), k_dec)
        o = o_inter + o_intra
        ms = jnp.mean(o * o, axis=-1, keepdims=True)
        o = o * lax.rsqrt(ms + NORM_EPS) * nw_ref[...] * _silu(g_ref[:, vs].astype(F32))
        o_ref[:, vs] = o.astype(BF16)


def _gla(u, small, wup_hi, wup_lo, bup, norm_w, tri, *, batch, seq):
    L = MIX_CHUNK
    nc = seq // L
    m = batch * seq

    def rows(b, c):
        return b * nc + c

    const = lambda shape: pl.BlockSpec(shape, lambda b, c: (0,) * len(shape))
    return pl.pallas_call(
        _gla_kernel,
        grid=(batch, nc),
        in_specs=[
            pl.BlockSpec((L, GLA_KEY_WIDTH), lambda b, c: (rows(b, c), 9)),
            pl.BlockSpec((L, GLA_KEY_WIDTH), lambda b, c: (rows(b, c), 10)),
            pl.BlockSpec((L, GLA_VALUE_WIDTH), lambda b, c: (rows(b, c), 2)),
            pl.BlockSpec((L, GLA_VALUE_WIDTH), lambda b, c: (rows(b, c), 3)),
            pl.BlockSpec((L, SMALL_WIDTH), lambda b, c: (rows(b, c), 0)),
            const((SMALL_WIDTH, GLA_KEY_WIDTH)),
            const((SMALL_WIDTH, GLA_KEY_WIDTH)),
            const((1, GLA_KEY_WIDTH)),
            const((1, GLA_HEAD_V)),
            const((L, L)),
        ],
        out_specs=pl.BlockSpec((L, GLA_VALUE_WIDTH), lambda b, c: (rows(b, c), 0)),
        out_shape=jax.ShapeDtypeStruct((m, GLA_VALUE_WIDTH), BF16),
        scratch_shapes=[
            pltpu.VMEM((L, GLA_KEY_WIDTH), F32),
            pltpu.VMEM((L, L), F32),
            pltpu.VMEM((GLA_HEADS, GLA_HEAD_V, GLA_HEAD_K), F32),
        ],
        compiler_params=pltpu.CompilerParams(
            dimension_semantics=("arbitrary", "arbitrary"),
            vmem_limit_bytes=VMEM_LIMIT_BYTES),
        name="gla",
    )(u, u, u, u, small, wup_hi, wup_lo, bup, norm_w, tri)


def _out_proj_kernel(ys_ref, yg_ref, x_ref, wo1_ref, wo2_ref, nw_ref, wrh_ref, wrl_ref,
                     rb_ref, stri_ref, h1_ref, n2_ref, idx_ref, wt_ref, rank_ref, cnt_ref,
                     run_ref):
    tm = x_ref.shape[0]

    @pl.when(pl.program_id(0) == 0)
    def _():
        run_ref[...] = jnp.zeros(run_ref.shape, F32)

    h = x_ref[...] + _dot(ys_ref[...], wo1_ref[...]) + _dot(yg_ref[...], wo2_ref[...])
    h1_ref[...] = h
    ms = jnp.mean(h * h, axis=-1, keepdims=True)
    n2 = h * lax.rsqrt(ms + NORM_EPS) * nw_ref[...]
    n2_ref[...] = _pack_bf16_pairs(n2)

    lane = lax.broadcasted_iota(jnp.int32, (tm, LANES), 1)
    lane_f = lane.astype(F32)
    logits = _dot_hilo(n2, wrh_ref[...], wrl_ref[...]) + rb_ref[...]
    work = jnp.where(lane < N_EXPERTS, logits, -jnp.inf)
    vals, idxs, sels = [], [], []
    for _ in range(TOP_K):
        top = jnp.max(work, axis=-1, keepdims=True)
        idx = jnp.min(jnp.where(work == top, lane_f, float(LANES)), axis=-1, keepdims=True)
        sel = lane_f == idx
        vals.append(top)
        idxs.append(idx)
        sels.append(sel)
        work = jnp.where(sel, -jnp.inf, work)
    exps = [jnp.exp(v - vals[0]) for v in vals]
    denom = exps[0] + exps[1] + exps[2] + exps[3]

    chosen = jnp.zeros((tm, LANES), F32)
    for sel in sels:
        chosen = chosen + jnp.where(sel, 1.0, 0.0)
    before = run_ref[0:1, :] + _dot(stri_ref[...], chosen.astype(BF16))

    idx_out = jnp.zeros((tm, LANES), F32)
    wt_out = jnp.zeros((tm, LANES), F32)
    rank_out = jnp.zeros((tm, LANES), F32)
    for j in range(TOP_K):
        rank_j = jnp.sum(jnp.where(sels[j], before, 0.0), axis=-1, keepdims=True)
        idx_out = jnp.where(lane == j, idxs[j], idx_out)
        wt_out = jnp.where(lane == j, exps[j] / denom, wt_out)
        rank_out = jnp.where(lane == j, rank_j, rank_out)
    idx_ref[...] = idx_out.astype(jnp.int32)
    wt_ref[...] = wt_out
    rank_ref[...] = rank_out.astype(jnp.int32)

    run_ref[...] = run_ref[...] + jnp.sum(chosen, axis=0, keepdims=True)
    cnt_ref[...] = run_ref[...].astype(jnp.int32)


def _out_proj(y_ssd, y_gla, x2, wo1, wo2, norm_w, wr_hi, wr_lo, rb, stri, *, tm):
    m = x2.shape[0]
    const = lambda shape: pl.BlockSpec(shape, lambda i: (0,) * len(shape))
    tile = lambda w: pl.BlockSpec((tm, w), lambda i: (i, 0))
    return pl.pallas_call(
        _out_proj_kernel,
        grid=(m // tm,),
        in_specs=[
            tile(SSD_WIDTH), tile(GLA_VALUE_WIDTH), tile(D_MODEL),
            const((SSD_WIDTH, D_MODEL)), const((GLA_VALUE_WIDTH, D_MODEL)),
            const((1, D_MODEL)),
            const((D_MODEL, LANES)), const((D_MODEL, LANES)), const((1, LANES)),
            const((tm, tm)),
        ],
        out_specs=[
            tile(D_MODEL), tile(PACKED_WIDTH), tile(LANES), tile(LANES), tile(LANES),
            const((SUBLANES, LANES)),
        ],
        out_shape=[
            jax.ShapeDtypeStruct((m, D_MODEL), F32),
            jax.ShapeDtypeStruct((m, PACKED_WIDTH), jnp.uint32),
            jax.ShapeDtypeStruct((m, LANES), jnp.int32),
            jax.ShapeDtypeStruct((m, LANES), F32),
            jax.ShapeDtypeStruct((m, LANES), jnp.int32),
            jax.ShapeDtypeStruct((SUBLANES, LANES), jnp.int32),
        ],
        scratch_shapes=[pltpu.VMEM((SUBLANES, LANES), F32)],
        compiler_params=pltpu.CompilerParams(
            dimension_semantics=("arbitrary",),
            vmem_limit_bytes=VMEM_LIMIT_BYTES),
        name="out_proj",
    )(y_ssd, y_gla, x2, wo1, wo2, norm_w, wr_hi, wr_lo, rb, stri)


def _dispatch_kernel(fill_ref, dest_ref, n2_ref, xg_hbm, zero_ref, sem, zsem):
    tm = TOKEN_TILE

    @pl.when(pl.program_id(0) == 0)
    def _():
        zero_ref[...] = jnp.zeros(zero_ref.shape, zero_ref.dtype)

        def fill(b):
            row = pl.multiple_of(b * ROW_BLOCK, ROW_BLOCK)
            return pltpu.make_async_copy(zero_ref, xg_hbm.at[pl.ds(row, ROW_BLOCK)], zsem)

        def start_fill(b, carry):
            @pl.when(fill_ref[b] > 0)
            def _():
                fill(b).start()
            return carry

        def wait_fill(b, carry):
            @pl.when(fill_ref[b] > 0)
            def _():
                fill(b).wait()
            return carry

        lax.fori_loop(0, fill_ref.shape[0], start_fill, 0)
        lax.fori_loop(0, fill_ref.shape[0], wait_fill, 0)

    def copy(r, row):
        return pltpu.make_async_copy(n2_ref.at[pl.ds(r, 1)], xg_hbm.at[pl.ds(row, 1)], sem)

    for r in range(tm):
        for j in range(TOP_K):
            copy(r, dest_ref[0, 0, TOP_K * r + j]).start(priority=j % 2)
    for r in range(tm):
        for j in range(TOP_K):
            copy(r, 0).wait()


def _dispatch(fill, dest3, n2p):
    n_tiles = dest3.shape[0]
    n_rows = fill.shape[0] * ROW_BLOCK
    grid_spec = pltpu.PrefetchScalarGridSpec(
        num_scalar_prefetch=1,
        grid=(n_tiles,),
        in_specs=[
            pl.BlockSpec((1, 1, TOP_K * TOKEN_TILE), lambda i, fl: (i, 0, 0),
                         memory_space=pltpu.SMEM),
            pl.BlockSpec((TOKEN_TILE, PACKED_WIDTH), lambda i, fl: (i, 0)),
        ],
        out_specs=pl.BlockSpec(memory_space=pl.ANY),
        scratch_shapes=[
            pltpu.VMEM((ROW_BLOCK, PACKED_WIDTH), jnp.uint32),
            pltpu.SemaphoreType.DMA(()),
            pltpu.SemaphoreType.DMA(()),
        ],
    )
    return pl.pallas_call(
        _dispatch_kernel,
        grid_spec=grid_spec,
        out_shape=jax.ShapeDtypeStruct((n_rows, PACKED_WIDTH), jnp.uint32),
        compiler_params=pltpu.CompilerParams(
            dimension_semantics=("arbitrary",), has_side_effects=True),
        name="dispatch",
    )(fill, dest3, n2p)


def _ffn_kernel(sbe_ref, nact_ref, src_ref, x_ref, wg_ref, wu_ref, wd_ref, bg_ref, bu_ref,
                bd_ref, o_ref, xb_ref, act_ref, wgb_ref, wub_ref, wdb_ref):
    del sbe_ref, src_ref
    i = pl.program_id(0)
    f = pl.program_id(1)
    nf = D_FF // FFN_TILE
    nact = nact_ref[i]

    full = nact == SUB_PER_SUPER

    def gate_up(sb):
        rows = slice(sb * ROW_BLOCK, (sb + 1) * ROW_BLOCK)
        xs = xb_ref[rows, :]
        gate = jnp.minimum(_dot(xs, wgb_ref[...]) + bg_ref[0], SWIGLU_LIMIT)
        up = jnp.clip(_dot(xs, wub_ref[...]) + bu_ref[0], -SWIGLU_LIMIT, SWIGLU_LIMIT)
        act_ref[f, rows, :] = ((up + 1.0) * gate * _sigmoid(SWIGLU_ALPHA * gate)).astype(BF16)

    def down(sb):
        rows = slice(sb * ROW_BLOCK, (sb + 1) * ROW_BLOCK)
        act = jnp.concatenate([act_ref[k, rows, :] for k in range(nf)], axis=1)
        out = _dot(act, wdb_ref[...]) + bd_ref[0]
        o_ref[rows, :] = _pack_bf16_pairs(out)

    @pl.when(jnp.logical_and(nact > 0, f == 0))
    def _():
        for sb in range(SUB_PER_SUPER):
            rows = slice(sb * ROW_BLOCK, (sb + 1) * ROW_BLOCK)
            lo, hi = _unpack_bf16_pairs(x_ref[rows, :])
            xb_ref[rows, 0:PACKED_WIDTH] = lo.astype(BF16)
            xb_ref[rows, PACKED_WIDTH:D_MODEL] = hi.astype(BF16)

    def cast_weights():
        wgb_ref[...] = wg_ref[0].astype(BF16)
        wub_ref[...] = wu_ref[0].astype(BF16)
        wdb_ref[pl.ds(pl.multiple_of(f * FFN_TILE, FFN_TILE), FFN_TILE), :] = (
            wd_ref[0].astype(BF16))

    @pl.when(full)
    def _():
        cast_weights()
        for sb in range(SUB_PER_SUPER):
            gate_up(sb)

    @pl.when(jnp.logical_and(nact > 0, jnp.logical_not(full)))
    def _():
        cast_weights()
        gate_up(0)

    for sb in range(1, SUB_PER_SUPER - 1):
        @pl.when(jnp.logical_and(sb < nact, jnp.logical_not(full)))
        def _():
            gate_up(sb)

    @pl.when(f == nf - 1)
    def _():
        @pl.when(full)
        def _():
            for sb in range(SUB_PER_SUPER):
                down(sb)

        for sb in range(SUB_PER_SUPER):
            rows = slice(sb * ROW_BLOCK, (sb + 1) * ROW_BLOCK)

            @pl.when(jnp.logical_and(sb < nact, jnp.logical_not(full)))
            def _():
                down(sb)

            @pl.when(sb >= nact)
            def _():
                o_ref[rows, :] = jnp.zeros((ROW_BLOCK, PACKED_WIDTH), jnp.uint32)


def _ffn(sb_expert, sb_nact, sb_src, xg, wgu, wd, bgu3, bd3):
    n_super = sb_expert.shape[0]
    nf = D_FF // FFN_TILE

    def tile(f, n, i):
        return jnp.where(n[i] > 0, f, nf - 1)

    grid_spec = pltpu.PrefetchScalarGridSpec(
        num_scalar_prefetch=3,
        grid=(n_super, nf),
        in_specs=[
            pl.BlockSpec((SUPER_ROWS, PACKED_WIDTH), lambda i, f, e, n, s: (s[i], 0)),
            pl.BlockSpec((1, D_MODEL, FFN_TILE), lambda i, f, e, n, s: (e[i], 0, tile(f, n, i))),
            pl.BlockSpec((1, D_MODEL, FFN_TILE),
                         lambda i, f, e, n, s: (e[i], 0, tile(f, n, i) + nf)),
            pl.BlockSpec((1, FFN_TILE, D_MODEL), lambda i, f, e, n, s: (e[i], tile(f, n, i), 0)),
            pl.BlockSpec((1, 1, FFN_TILE), lambda i, f, e, n, s: (e[i], 0, tile(f, n, i))),
            pl.BlockSpec((1, 1, FFN_TILE), lambda i, f, e, n, s: (e[i], 0, tile(f, n, i) + nf)),
            pl.BlockSpec((1, 1, D_MODEL), lambda i, f, e, n, s: (e[i], 0, 0)),
        ],
        out_specs=pl.BlockSpec((SUPER_ROWS, PACKED_WIDTH), lambda i, f, e, n, s: (i, 0)),
        scratch_shapes=[
            pltpu.VMEM((SUPER_ROWS, D_MODEL), BF16),
            pltpu.VMEM((D_FF // FFN_TILE, SUPER_ROWS, FFN_TILE), BF16),
            pltpu.VMEM((D_MODEL, FFN_TILE), BF16),
            pltpu.VMEM((D_MODEL, FFN_TILE), BF16),
            pltpu.VMEM((D_FF, D_MODEL), BF16),
        ],
    )
    return pl.pallas_call(
        _ffn_kernel,
        grid_spec=grid_spec,
        out_shape=jax.ShapeDtypeStruct(xg.shape, jnp.uint32),
        compiler_params=pltpu.CompilerParams(
            dimension_semantics=("arbitrary", "arbitrary"),
            vmem_limit_bytes=VMEM_LIMIT_BYTES),
        name="ffn",
    )(sb_expert, sb_nact, sb_src, xg, wgu, wgu, wd, bgu3, bgu3, bd3)


def _combine_kernel(dcur_ref, dnxt_ref, wt_ref, h1_ref, fw_ref, eo_hbm, o_ref, buf, sem):
    tm = TOKEN_TILE
    i = pl.program_id(0)
    n = pl.num_programs(0)
    slot = lax.rem(i, 2)

    def copy(row, s, r, j):
        return pltpu.make_async_copy(
            eo_hbm.at[pl.ds(row, 1)], buf.at[s, j, pl.ds(r, 1)], sem.at[s])

    def start_all(dref, s):
        for r in range(tm):
            for j in range(TOP_K):
                copy(dref[0, 0, TOP_K * r + j], s, r, j).start(priority=j % 2)

    def wait_all(s):
        for r in range(tm):
            for j in range(TOP_K):
                copy(0, s, r, j).wait()

    @pl.when(i == 0)
    def _():
        start_all(dcur_ref, 0)

    for s in range(2):
        @pl.when(jnp.logical_and(i + 1 < n, slot != s))
        def _():
            start_all(dnxt_ref, s)

    for s in range(2):
        @pl.when(slot == s)
        def _():
            wait_all(s)

    wt = wt_ref[...]
    half = PACKED_WIDTH
    y_lo = h1_ref[:, 0:half]
    y_hi = h1_ref[:, half:D_MODEL]
    for j in range(TOP_K):
        lo, hi = _unpack_bf16_pairs(buf[slot, j])
        y_lo = y_lo + lo * wt[:, j:j + 1]
        y_hi = y_hi + hi * wt[:, j:j + 1]
    ssq = (jnp.sum(y_lo * y_lo, axis=-1, keepdims=True)
           + jnp.sum(y_hi * y_hi, axis=-1, keepdims=True))
    scale = lax.rsqrt(ssq * (1.0 / D_MODEL) + NORM_EPS)
    o_ref[:, 0:half] = y_lo * scale * fw_ref[:, 0:half]
    o_ref[:, half:D_MODEL] = y_hi * scale * fw_ref[:, half:D_MODEL]


def _combine(dest3, wt, h1, final_w, eo):
    n_tiles = dest3.shape[0]
    tm = TOKEN_TILE
    m = h1.shape[0]
    smem = lambda fn: pl.BlockSpec((1, 1, TOP_K * tm), fn, memory_space=pltpu.SMEM)
    return pl.pallas_call(
        _combine_kernel,
        grid=(n_tiles,),
        in_specs=[
            smem(lambda i: (i, 0, 0)),
            smem(lambda i: (jnp.minimum(i + 1, n_tiles - 1), 0, 0)),
            pl.BlockSpec((tm, LANES), lambda i: (i, 0)),
            pl.BlockSpec((tm, D_MODEL), lambda i: (i, 0)),
            pl.BlockSpec((1, D_MODEL), lambda i: (0, 0)),
            pl.BlockSpec(memory_space=pl.ANY),
        ],
        out_specs=pl.BlockSpec((tm, D_MODEL), lambda i: (i, 0)),
        out_shape=jax.ShapeDtypeStruct((m, D_MODEL), F32),
        scratch_shapes=[
            pltpu.VMEM((2, TOP_K, tm, PACKED_WIDTH), jnp.uint32),
            pltpu.SemaphoreType.DMA((2,)),
        ],
        compiler_params=pltpu.CompilerParams(
            dimension_semantics=("arbitrary",),
            vmem_limit_bytes=VMEM_LIMIT_BYTES),
        name="combine",
    )(dest3, dest3, wt, h1, final_w, eo)


def _pad_lanes(v, width=LANES):
    return jnp.pad(v, [(0, 0)] * (v.ndim - 1) + [(0, width - v.shape[-1])])


def _layer(h, norm_mix_w, in_proj_w, conv_w, conv_b, dt_bias, a_log, d_skip, ssd_norm_w,
           gate_up_w, gate_up_b, gla_norm_w, out_proj_w, norm_ffn_w, router_w, router_b,
           w_gate_up, b_gate_up, w_down, b_down, out_norm_w):
    batch, seq, _ = h.shape
    m = batch * seq
    x2 = h.reshape(m, D_MODEL)
    L = MIX_CHUNK

    w = in_proj_w.T
    o_xbc = SSD_WIDTH
    o_dt = o_xbc + SSD_CONV_DIM
    o_q = o_dt + SSD_HEADS
    o_k = o_q + GLA_KEY_WIDTH
    o_v = o_k + GLA_KEY_WIDTH
    o_g = o_v + GLA_VALUE_WIDTH
    o_gl = o_g + GLA_VALUE_WIDTH
    tn = 1024
    seg_starts = (0, o_xbc, o_v, o_g, o_xbc + SSD_WIDTH, o_q, o_k)
    seg_widths = (SSD_WIDTH, SSD_WIDTH, GLA_VALUE_WIDTH, GLA_VALUE_WIDTH, SSD_BC_WIDTH,
                  GLA_KEY_WIDTH, GLA_KEY_WIDTH)
    row_offs = jnp.array([s + t for s, wd_ in zip(seg_starts, seg_widths)
                          for t in range(0, wd_, tn)], jnp.int32)
    n_small = SSD_HEADS + GLA_GATE_RANK
    wt_small = jnp.concatenate([w[o_dt:o_q], w[o_gl:o_gl + GLA_GATE_RANK],
                                jnp.zeros((SMALL_WIDTH - n_small, D_MODEL), F32)], axis=0)
    ws_hi, ws_lo = _split2(wt_small)

    u, small = _in_proj(row_offs, x2, norm_mix_w.reshape(1, D_MODEL), w.astype(BF16),
                        ws_hi, ws_lo, tm=min(1024, m), tn=tn)

    tri = jnp.tril(jnp.ones((L, L), F32)).astype(BF16)
    head_of_col = jnp.arange(SSD_WIDTH) // SSD_HEAD_DIM
    expand = (jnp.arange(LANES)[:, None] == head_of_col[None, :]).astype(BF16)
    e2 = jnp.concatenate([expand, expand], axis=0)
    y_ssd = _ssd(u, small, conv_w, conv_b.reshape(1, -1),
                 _pad_lanes(dt_bias.reshape(1, -1)), _pad_lanes(a_log.reshape(1, -1)),
                 jnp.repeat(d_skip, SSD_HEAD_DIM).reshape(1, -1), ssd_norm_w.reshape(1, -1),
                 e2, tri, batch=batch, seq=seq)

    wup = jnp.zeros((SMALL_WIDTH, GLA_KEY_WIDTH), F32)
    wup = wup.at[SMALL_GATE_OFF:SMALL_GATE_OFF + GLA_GATE_RANK].set(gate_up_w)
    wup_hi, wup_lo = _split2(wup)
    y_gla = _gla(u, small, wup_hi, wup_lo, gate_up_b.reshape(1, -1), gla_norm_w.reshape(1, -1),
                 tri, batch=batch, seq=seq)

    tm = TOKEN_TILE
    wr_hi, wr_lo = _split2(_pad_lanes(router_w))
    stri = jnp.tril(jnp.ones((tm, tm), F32), k=-1).astype(BF16)
    wo = out_proj_w.astype(BF16)
    h1, n2, top_idx, top_w, rank, cnt = _out_proj(
        y_ssd, y_gla, x2, wo[:SSD_WIDTH], wo[SSD_WIDTH:], norm_ffn_w.reshape(1, -1),
        wr_hi, wr_lo, _pad_lanes(router_b.reshape(1, -1)), stri, tm=tm)

    counts = cnt[0, :N_EXPERTS]
    n_sb = (counts + SUPER_ROWS - 1) // SUPER_ROWS
    sb_end = jnp.cumsum(n_sb)
    sb_start = sb_end - n_sb
    dest = (sb_start * SUPER_ROWS)[top_idx[:, :TOP_K]] + rank[:, :TOP_K]
    dest3 = dest.astype(jnp.int32).reshape(m // tm, 1, TOP_K * tm)

    n_super = (m * TOP_K + N_EXPERTS * (SUPER_ROWS - 1)) // SUPER_ROWS
    sb_ids = jnp.arange(n_super)
    total_sb = sb_end[-1]
    sb_live = jnp.minimum(sb_ids, total_sb - 1)
    sb_expert = jnp.minimum(jnp.searchsorted(sb_end, sb_live, side="right"), N_EXPERTS - 1)
    rows_left = counts[sb_expert] - (sb_live - sb_start[sb_expert]) * SUPER_ROWS
    sb_nact = jnp.where(sb_ids < total_sb,
                        (jnp.clip(rows_left, 0, SUPER_ROWS) + ROW_BLOCK - 1) // ROW_BLOCK, 0)

    sb_rows = jnp.where(sb_ids < total_sb, jnp.clip(rows_left, 0, SUPER_ROWS), 0)
    sub_rows = sb_rows[:, None] - jnp.arange(SUB_PER_SUPER)[None, :] * ROW_BLOCK
    fill = (sub_rows < ROW_BLOCK).astype(jnp.int32).reshape(-1)
    xg = _dispatch(fill, dest3, n2)
    eo = _ffn(sb_expert.astype(jnp.int32), sb_nact.astype(jnp.int32),
              sb_live.astype(jnp.int32), xg, w_gate_up, w_down,
              b_gate_up.reshape(N_EXPERTS, 1, 2 * D_FF), b_down.reshape(N_EXPERTS, 1, D_MODEL))
    out = _combine(dest3, top_w, h1, out_norm_w.reshape(1, -1), eo)
    return out.reshape(batch, seq, D_MODEL)


def kernel(x, norm_mix_w, in_proj_w, conv_w, conv_b, dt_bias, a_log, d_skip, ssd_norm_w,
           gate_up_w, gate_up_b, gla_norm_w, out_proj_w, norm_ffn_w, router_w, router_b,
           w_gate_up, b_gate_up, w_down, b_down, final_norm_w):
    assert norm_mix_w.shape[0] == 1
    return _layer(x, norm_mix_w[0], in_proj_w[0], conv_w[0], conv_b[0], dt_bias[0], a_log[0],
                  d_skip[0], ssd_norm_w[0], gate_up_w[0], gate_up_b[0], gla_norm_w[0],
                  out_proj_w[0], norm_ffn_w[0], router_w[0], router_b[0], w_gate_up[0],
                  b_gate_up[0], w_down[0], b_down[0], final_norm_w)
```

```python
import functools

import jax
import jax.numpy as jnp
from jax import lax
from jax.experimental import pallas as pl
from jax.experimental.pallas import tpu as pltpu

F32 = jnp.float32
BF16 = jnp.bfloat16

D_MODEL = 2048
NORM_EPS = 1e-5

SSD_WIDTH = 2048
SSD_HEAD_DIM = 64
SSD_HEADS = 32
SSD_GROUPS = 4
SSD_STATE = 128
CONV_WIDTH = 4
SSD_GROUP_WIDTH = SSD_WIDTH // SSD_GROUPS
SSD_BC_WIDTH = 2 * SSD_GROUPS * SSD_STATE
SSD_CONV_DIM = SSD_WIDTH + SSD_BC_WIDTH

GLA_HEADS = 4
GLA_KEY_WIDTH = 1024
GLA_VALUE_WIDTH = 2048
GLA_HEAD_K = 256
GLA_HEAD_V = 512
GLA_GATE_RANK = 16
GLA_GATE_NORMALIZER = 16.0

N_EXPERTS = 32
TOP_K = 4
D_FF = 2048
SWIGLU_LIMIT = 7.0
SWIGLU_ALPHA = 1.702

LANES = 128
SUBLANES = 8
VMEM_LIMIT_BYTES = 56 * 1024 * 1024

U_WIDTH = 11264
SMALL_WIDTH = LANES
SMALL_DT_OFF = 0
SMALL_GATE_OFF = SSD_HEADS

MIX_CHUNK = 128
GLA_SUB = 16
GLA_EXP_CLAMP = 80.0

ROW_BLOCK = 256
SUPER_ROWS = 768
SUB_PER_SUPER = SUPER_ROWS // ROW_BLOCK
PACKED_WIDTH = D_MODEL // 2
FFN_TILE = 512
TOKEN_TILE = 256


def _split2(v):
    hi = v.astype(BF16)
    lo = (v - hi.astype(F32)).astype(BF16)
    return hi, lo


def _split3(v):
    hi = v.astype(BF16)
    r = v - hi.astype(F32)
    mid = r.astype(BF16)
    lo = (r - mid.astype(F32)).astype(BF16)
    return hi, mid, lo


def _dot(a, b):
    return jnp.dot(a, b, preferred_element_type=F32)


def _dot_nt(a, b):
    return lax.dot_general(a, b, (((1,), (1,)), ((), ())), preferred_element_type=F32)


def _dot_exact_lhs(mat01, v):
    hi, mid, lo = _split3(v)
    return _dot(mat01, hi) + _dot(mat01, mid) + _dot(mat01, lo)


def _dot_hilo(a, b_hi, b_lo):
    a_hi, a_lo = _split2(a)
    return _dot(a_hi, b_hi) + _dot(a_lo, b_hi) + _dot(a_hi, b_lo)


def _pack_bf16_pairs(v):
    c = v.shape[1] // 2
    lo = lax.bitcast_convert_type(v[:, :c].astype(BF16).astype(F32), jnp.uint32)
    hi = lax.bitcast_convert_type(v[:, c:].astype(BF16).astype(F32), jnp.uint32)
    return (lo >> 16) | (hi & jnp.uint32(0xFFFF0000))


def _unpack_bf16_pairs(w):
    lo = lax.bitcast_convert_type(w << 16, F32)
    hi = lax.bitcast_convert_type(w & jnp.uint32(0xFFFF0000), F32)
    return lo, hi


def _sigmoid(v):
    return 0.5 * jnp.tanh(0.5 * v) + 0.5


def _silu(v):
    return v * _sigmoid(v)


def _softplus(v):
    return jnp.maximum(v, 0.0) + jnp.log(1.0 + jnp.exp(-jnp.abs(v)))


def _log_sigmoid(v):
    return jnp.minimum(v, 0.0) - jnp.log(1.0 + jnp.exp(-jnp.abs(v)))


def _in_proj_kernel(offs_ref, x_ref, nw_ref, wt_ref, wsh_ref, wsl_ref, u_ref, small_ref,
                    n1_ref):
    del offs_ref
    @pl.when(pl.program_id(1) == 0)
    def _():
        x = x_ref[...]
        ms = jnp.mean(x * x, axis=-1, keepdims=True)
        n1 = x * lax.rsqrt(ms + NORM_EPS) * nw_ref[...]
        n1_ref[...] = n1.astype(BF16)
        n1_hi, n1_lo = _split2(n1)
        small_ref[...] = (_dot_nt(n1_hi, wsh_ref[...]) + _dot_nt(n1_lo, wsh_ref[...])
                          + _dot_nt(n1_hi, wsl_ref[...]))

    u_ref[...] = _dot_nt(n1_ref[...], wt_ref[...]).astype(BF16)


def _in_proj(row_offs, x2, norm_w, wt, ws_hi, ws_lo, *, tm, tn):
    m = x2.shape[0]
    grid_spec = pltpu.PrefetchScalarGridSpec(
        num_scalar_prefetch=1,
        grid=(m // tm, U_WIDTH // tn),
        in_specs=[
            pl.BlockSpec((tm, D_MODEL), lambda i, j, o: (i, 0)),
            pl.BlockSpec((1, D_MODEL), lambda i, j, o: (0, 0)),
            pl.BlockSpec((pl.Element(tn), pl.Element(D_MODEL)),
                         lambda i, j, o: (pl.multiple_of(o[j], 2 * SUBLANES), 0)),
            pl.BlockSpec((SMALL_WIDTH, D_MODEL), lambda i, j, o: (0, 0)),
            pl.BlockSpec((SMALL_WIDTH, D_MODEL), lambda i, j, o: (0, 0)),
        ],
        out_specs=[
            pl.BlockSpec((tm, tn), lambda i, j, o: (i, j)),
            pl.BlockSpec((tm, SMALL_WIDTH), lambda i, j, o: (i, 0)),
        ],
        scratch_shapes=[pltpu.VMEM((tm, D_MODEL), BF16)],
    )
    return pl.pallas_call(
        _in_proj_kernel,
        grid_spec=grid_spec,
        out_shape=[
            jax.ShapeDtypeStruct((m, U_WIDTH), BF16),
            jax.ShapeDtypeStruct((m, SMALL_WIDTH), F32),
        ],
        compiler_params=pltpu.CompilerParams(
            dimension_semantics=("arbitrary", "arbitrary"),
            vmem_limit_bytes=VMEM_LIMIT_BYTES),
        name="in_proj",
    )(row_offs, x2, norm_w, wt, ws_hi, ws_lo)


def _ssd_kernel(z_ref, x_ref, bc_ref, sm_ref, convw_ref, convb_ref, dtb_ref, alog_ref,
                dsk_ref, nw_ref, e2_ref, tri_ref, shift_ref, o_ref,
                ext_ref, xc_ref, xd_ref, xdd_ref, y_ref, state_ref):
    L = MIX_CHUNK
    G, N, P = SSD_GROUPS, SSD_STATE, SSD_HEAD_DIM
    GW = SSD_GROUP_WIDTH

    @pl.when(pl.program_id(1) == 0)
    def _():
        ext_ref[0:L, :] = jnp.zeros((L, SSD_CONV_DIM), BF16)
        state_ref[...] = jnp.zeros(state_ref.shape, F32)

    ext_ref[L:2 * L, 0:SSD_WIDTH] = x_ref[...]
    ext_ref[L:2 * L, SSD_WIDTH:SSD_CONV_DIM] = bc_ref[...]
    slab = 512
    for s in range(SSD_CONV_DIM // slab):
        cols = slice(s * slab, (s + 1) * slab)
        acc = (convb_ref[:, cols]
               + convw_ref[CONV_WIDTH - 1:CONV_WIDTH, cols] * ext_ref[L:2 * L, cols].astype(F32))
        for k in range(1, CONV_WIDTH):
            w = CONV_WIDTH - 1 - k
            acc = acc + convw_ref[w:w + 1, cols] * _dot(shift_ref[k - 1], ext_ref[:, cols])
        xc_ref[:, cols] = _silu(acc)
    ext_ref[0:L, :] = ext_ref[L:2 * L, :]

    lane = lax.broadcasted_iota(jnp.int32, (1, LANES), 1)
    a = jnp.where(lane < SSD_HEADS, -jnp.exp(alog_ref[...]), 0.0)
    dt = _softplus(sm_ref[...] + dtb_ref[...])
    la = _dot_exact_lhs(tri_ref[...], dt * a)
    la_t = la.T
    la_end = la[L - 1:L, :]

    def hilo(v):
        hi, lo = _split2(v)
        return jnp.concatenate([hi, lo], axis=1)

    def expand(v_hl, cols):
        return _dot(v_hl, e2_ref[:, cols])

    dt_hl = hilo(dt)
    ela_hl = hilo(jnp.exp(la))
    dte_hl = hilo(jnp.exp(la_end - la))
    cd_hl = hilo(jnp.broadcast_to(jnp.exp(la_end), (SUBLANES, LANES)))

    for s in range(SSD_WIDTH // slab):
        cols = slice(s * slab, (s + 1) * slab)
        xd = xc_ref[:, cols] * expand(dt_hl, cols)
        xd_ref[:, cols] = xd
        xdd_ref[:, cols] = (xd * expand(dte_hl, cols)).astype(BF16)

    row = lax.broadcasted_iota(jnp.int32, (L, L), 0)
    col = lax.broadcasted_iota(jnp.int32, (L, L), 1)
    causal = row >= col
    lane_l = lax.broadcasted_iota(jnp.int32, (L, LANES), 1)

    for g in range(G):
        bm = xc_ref[:, SSD_WIDTH + g * N:SSD_WIDTH + (g + 1) * N]
        cm = xc_ref[:, SSD_WIDTH + G * N + g * N:SSD_WIDTH + G * N + (g + 1) * N]
        cm_b = cm.astype(BF16)
        cb = _dot_nt(cm_b, bm.astype(BF16))
        gcols = slice(g * GW, (g + 1) * GW)
        y_ref[:, gcols] = (_dot(cm_b, state_ref[g].astype(BF16)) * expand(ela_hl, gcols)
                           + dsk_ref[:, gcols] * xc_ref[:, gcols])
        for p in range(GW // LANES):
            h0 = g * (GW // P) + 2 * p
            ms = []
            for h in (h0, h0 + 1):
                seg = la[:, h:h + 1] - la_t[h:h + 1, :]
                dec = jnp.where(causal, jnp.exp(jnp.minimum(seg, 0.0)), 0.0)
                ms.append((cb * dec).astype(BF16))
            pcols = slice(g * GW + p * LANES, g * GW + (p + 1) * LANES)
            xp = xd_ref[:, pcols]
            rhs = jnp.concatenate([jnp.where(lane_l < P, xp, 0.0),
                                   jnp.where(lane_l >= P, xp, 0.0)], axis=0).astype(BF16)
            y_ref[:, pcols] = y_ref[:, pcols] + _dot(jnp.concatenate(ms, axis=1), rhs)
        upd = _dot(bm.T.astype(BF16), xdd_ref[:, gcols])
        state_ref[g] = expand(cd_hl, gcols)[0:1, :] * state_ref[g] + upd

    for g in range(G):
        gcols = slice(g * GW, (g + 1) * GW)
        y = y_ref[:, gcols] * _silu(z_ref[:, gcols].astype(F32))
        ms = jnp.mean(y * y, axis=-1, keepdims=True)
        o_ref[:, gcols] = (y * lax.rsqrt(ms + NORM_EPS) * nw_ref[:, gcols]).astype(BF16)


def _ssd(u, small, conv_w, conv_b, dtb, alog, dsk_e, norm_w, e2, tri, *, batch, seq):
    L = MIX_CHUNK
    nc = seq // L
    m = batch * seq
    t_idx = jnp.arange(L)[None, :, None]
    j_idx = jnp.arange(2 * L)[None, None, :]
    k_idx = jnp.arange(1, CONV_WIDTH)[:, None, None]
    shift = (j_idx == L + t_idx - k_idx).astype(BF16)

    def rows(b, c):
        return b * nc + c

    const = lambda shape: pl.BlockSpec(shape, lambda b, c: (0,) * len(shape))
    return pl.pallas_call(
        _ssd_kernel,
        grid=(batch, nc),
        in_specs=[
            pl.BlockSpec((L, SSD_WIDTH), lambda b, c: (rows(b, c), 0)),
            pl.BlockSpec((L, SSD_WIDTH), lambda b, c: (rows(b, c), 1)),
            pl.BlockSpec((L, SSD_BC_WIDTH), lambda b, c: (rows(b, c), 8)),
            pl.BlockSpec((L, SMALL_WIDTH), lambda b, c: (rows(b, c), 0)),
            const((CONV_WIDTH, SSD_CONV_DIM)),
            const((1, SSD_CONV_DIM)),
            const((1, LANES)),
            const((1, LANES)),
            const((1, SSD_WIDTH)),
            const((1, SSD_WIDTH)),
            const((2 * LANES, SSD_WIDTH)),
            const((L, L)),
            const((CONV_WIDTH - 1, L, 2 * L)),
        ],
        out_specs=pl.BlockSpec((L, SSD_WIDTH), lambda b, c: (rows(b, c), 0)),
        out_shape=jax.ShapeDtypeStruct((m, SSD_WIDTH), BF16),
        scratch_shapes=[
            pltpu.VMEM((2 * L, SSD_CONV_DIM), BF16),
            pltpu.VMEM((L, SSD_CONV_DIM), F32),
            pltpu.VMEM((L, SSD_WIDTH), F32),
            pltpu.VMEM((L, SSD_WIDTH), BF16),
            pltpu.VMEM((L, SSD_WIDTH), F32),
            pltpu.VMEM((SSD_GROUPS, SSD_STATE, SSD_GROUP_WIDTH), F32),
        ],
        compiler_params=pltpu.CompilerParams(
            dimension_semantics=("arbitrary", "arbitrary"),
            vmem_limit_bytes=VMEM_LIMIT_BYTES),
        name="ssd",
    )(u, u, u, small, conv_w, conv_b, dtb, alog, dsk_e, norm_w, e2, tri, shift)


def _gla_kernel(q_ref, k_ref, v_ref, g_ref, sm_ref, wuh_ref, wul_ref, bup_ref, nw_ref,
                tri_ref, o_ref, cum_ref, s_ref, st_ref):
    L, C = MIX_CHUNK, GLA_SUB
    K, V = GLA_HEAD_K, GLA_HEAD_V

    @pl.when(pl.program_id(1) == 0)
    def _():
        st_ref[...] = jnp.zeros(st_ref.shape, F32)

    pre = _dot_hilo(sm_ref[...], wuh_ref[...], wul_ref[...]) + bup_ref[...]
    gk = _log_sigmoid(pre) * (1.0 / GLA_GATE_NORMALIZER)
    cum_ref[...] = _dot_exact_lhs(tri_ref[...], gk)

    row_c = lax.broadcasted_iota(jnp.int32, (C, L), 0)
    col_c = lax.broadcasted_iota(jnp.int32, (C, L), 1)
    for h in range(GLA_HEADS):
        ks = slice(h * K, (h + 1) * K)
        vs = slice(h * V, (h + 1) * V)
        cum = cum_ref[:, ks]
        cum_last = cum_ref[L - 1:L, ks]
        qh = q_ref[:, ks].astype(F32) * (K ** -0.5)
        kh = k_ref[:, ks].astype(F32)
        vh = v_ref[:, vs]
        for blk in range(L // C):
            rows = slice(blk * C, (blk + 1) * C)
            if blk == 0:
                ref_row = jnp.zeros((1, K), F32)
            else:
                ref_row = cum_ref[blk * C - 1:blk * C, ks]
            q_b = (qh[rows] * jnp.exp(cum_ref[rows, ks] - ref_row)).astype(BF16)
            n = (blk + 1) * C
            k_b = (kh[0:n] * jnp.exp(jnp.minimum(ref_row - cum[0:n], GLA_EXP_CLAMP))).astype(BF16)
            if n < L:
                k_b = jnp.concatenate([k_b, jnp.zeros((L - n, K), BF16)], axis=0)
            s_b = _dot_nt(q_b, k_b)
            s_ref[rows, :] = jnp.where(col_c <= row_c + blk * C, s_b, 0.0)
        o_intra = _dot(s_ref[...].astype(BF16), vh)
        st = st_ref[h]
        o_inter = _dot_nt((qh * jnp.exp(cum)).astype(BF16), st.astype(BF16))
        k_dec = (kh * jnp.exp(cum_last - cum)).astype(BF16)
        st_ref[h] = st * jnp.exp(cum_last) + _dot(vh.astype(F32).T.astype(BF16), k_dec)
        o = o_inter + o_intra
        ms = jnp.mean(o * o, axis=-1, keepdims=True)
        o = o * lax.rsqrt(ms + NORM_EPS) * nw_ref[...] * _silu(g_ref[:, vs].astype(F32))
        o_ref[:, vs] = o.astype(BF16)


def _gla(u, small, wup_hi, wup_lo, bup, norm_w, tri, *, batch, seq):
    L = MIX_CHUNK
    nc = seq // L
    m = batch * seq

    def rows(b, c):
        return b * nc + c

    const = lambda shape: pl.BlockSpec(shape, lambda b, c: (0,) * len(shape))
    return pl.pallas_call(
        _gla_kernel,
        grid=(batch, nc),
        in_specs=[
            pl.BlockSpec((L, GLA_KEY_WIDTH), lambda b, c: (rows(b, c), 9)),
            pl.BlockSpec((L, GLA_KEY_WIDTH), lambda b, c: (rows(b, c), 10)),
            pl.BlockSpec((L, GLA_VALUE_WIDTH), lambda b, c: (rows(b, c), 2)),
            pl.BlockSpec((L, GLA_VALUE_WIDTH), lambda b, c: (rows(b, c), 3)),
            pl.BlockSpec((L, SMALL_WIDTH), lambda b, c: (rows(b, c), 0)),
            const((SMALL_WIDTH, GLA_KEY_WIDTH)),
            const((SMALL_WIDTH, GLA_KEY_WIDTH)),
            const((1, GLA_KEY_WIDTH)),
            const((1, GLA_HEAD_V)),
            const((L, L)),
        ],
        out_specs=pl.BlockSpec((L, GLA_VALUE_WIDTH), lambda b, c: (rows(b, c), 0)),
        out_shape=jax.ShapeDtypeStruct((m, GLA_VALUE_WIDTH), BF16),
        scratch_shapes=[
            pltpu.VMEM((L, GLA_KEY_WIDTH), F32),
            pltpu.VMEM((L, L), F32),
            pltpu.VMEM((GLA_HEADS, GLA_HEAD_V, GLA_HEAD_K), F32),
        ],
        compiler_params=pltpu.CompilerParams(
            dimension_semantics=("arbitrary", "arbitrary"),
            vmem_limit_bytes=VMEM_LIMIT_BYTES),
        name="gla",
    )(u, u, u, u, small, wup_hi, wup_lo, bup, norm_w, tri)


def _out_proj_kernel(ys_ref, yg_ref, x_ref, wo1_ref, wo2_ref, nw_ref, wrh_ref, wrl_ref,
                     rb_ref, stri_ref, h1_ref, n2_ref, idx_ref, wt_ref, rank_ref, cnt_ref,
                     run_ref):
    tm = x_ref.shape[0]

    @pl.when(pl.program_id(0) == 0)
    def _():
        run_ref[...] = jnp.zeros(run_ref.shape, F32)

    h = x_ref[...] + _dot(ys_ref[...], wo1_ref[...]) + _dot(yg_ref[...], wo2_ref[...])
    h1_ref[...] = h
    ms = jnp.mean(h * h, axis=-1, keepdims=True)
    n2 = h * lax.rsqrt(ms + NORM_EPS) * nw_ref[...]
    n2_ref[...] = _pack_bf16_pairs(n2)

    lane = lax.broadcasted_iota(jnp.int32, (tm, LANES), 1)
    lane_f = lane.astype(F32)
    logits = _dot_hilo(n2, wrh_ref[...], wrl_ref[...]) + rb_ref[...]
    work = jnp.where(lane < N_EXPERTS, logits, -jnp.inf)
    vals, idxs, sels = [], [], []
    for _ in range(TOP_K):
        top = jnp.max(work, axis=-1, keepdims=True)
        idx = jnp.min(jnp.where(work == top, lane_f, float(LANES)), axis=-1, keepdims=True)
        sel = lane_f == idx
        vals.append(top)
        idxs.append(idx)
        sels.append(sel)
        work = jnp.where(sel, -jnp.inf, work)
    exps = [jnp.exp(v - vals[0]) for v in vals]
    denom = exps[0] + exps[1] + exps[2] + exps[3]

    chosen = jnp.zeros((tm, LANES), F32)
    for sel in sels:
        chosen = chosen + jnp.where(sel, 1.0, 0.0)
    before = run_ref[0:1, :] + _dot(stri_ref[...], chosen.astype(BF16))

    idx_out = jnp.zeros((tm, LANES), F32)
    wt_out = jnp.zeros((tm, LANES), F32)
    rank_out = jnp.zeros((tm, LANES), F32)
    for j in range(TOP_K):
        rank_j = jnp.sum(jnp.where(sels[j], before, 0.0), axis=-1, keepdims=True)
        idx_out = jnp.where(lane == j, idxs[j], idx_out)
        wt_out = jnp.where(lane == j, exps[j] / denom, wt_out)
        rank_out = jnp.where(lane == j, rank_j, rank_out)
    idx_ref[...] = idx_out.astype(jnp.int32)
    wt_ref[...] = wt_out
    rank_ref[...] = rank_out.astype(jnp.int32)

    run_ref[...] = run_ref[...] + jnp.sum(chosen, axis=0, keepdims=True)
    cnt_ref[...] = run_ref[...].astype(jnp.int32)


def _out_proj(y_ssd, y_gla, x2, wo1, wo2, norm_w, wr_hi, wr_lo, rb, stri, *, tm):
    m = x2.shape[0]
    const = lambda shape: pl.BlockSpec(shape, lambda i: (0,) * len(shape))
    tile = lambda w: pl.BlockSpec((tm, w), lambda i: (i, 0))
    return pl.pallas_call(
        _out_proj_kernel,
        grid=(m // tm,),
        in_specs=[
            tile(SSD_WIDTH), tile(GLA_VALUE_WIDTH), tile(D_MODEL),
            const((SSD_WIDTH, D_MODEL)), const((GLA_VALUE_WIDTH, D_MODEL)),
            const((1, D_MODEL)),
            const((D_MODEL, LANES)), const((D_MODEL, LANES)), const((1, LANES)),
            const((tm, tm)),
        ],
        out_specs=[
            tile(D_MODEL), tile(PACKED_WIDTH), tile(LANES), tile(LANES), tile(LANES),
            const((SUBLANES, LANES)),
        ],
        out_shape=[
            jax.ShapeDtypeStruct((m, D_MODEL), F32),
            jax.ShapeDtypeStruct((m, PACKED_WIDTH), jnp.uint32),
            jax.ShapeDtypeStruct((m, LANES), jnp.int32),
            jax.ShapeDtypeStruct((m, LANES), F32),
            jax.ShapeDtypeStruct((m, LANES), jnp.int32),
            jax.ShapeDtypeStruct((SUBLANES, LANES), jnp.int32),
        ],
        scratch_shapes=[pltpu.VMEM((SUBLANES, LANES), F32)],
        compiler_params=pltpu.CompilerParams(
            dimension_semantics=("arbitrary",),
            vmem_limit_bytes=VMEM_LIMIT_BYTES),
        name="out_proj",
    )(y_ssd, y_gla, x2, wo1, wo2, norm_w, wr_hi, wr_lo, rb, stri)


def _dispatch_kernel(fill_ref, dest_ref, n2_ref, xg_hbm, zero_ref, sem, zsem):
    tm = TOKEN_TILE

    @pl.when(pl.program_id(0) == 0)
    def _():
        zero_ref[...] = jnp.zeros(zero_ref.shape, zero_ref.dtype)

        def fill(b):
            row = pl.multiple_of(b * ROW_BLOCK, ROW_BLOCK)
            return pltpu.make_async_copy(zero_ref, xg_hbm.at[pl.ds(row, ROW_BLOCK)], zsem)

        def start_fill(b, carry):
            @pl.when(fill_ref[b] > 0)
            def _():
                fill(b).start()
            return carry

        def wait_fill(b, carry):
            @pl.when(fill_ref[b] > 0)
            def _():
                fill(b).wait()
            return carry

        lax.fori_loop(0, fill_ref.shape[0], start_fill, 0)
        lax.fori_loop(0, fill_ref.shape[0], wait_fill, 0)

    def copy(r, row):
        return pltpu.make_async_copy(n2_ref.at[pl.ds(r, 1)], xg_hbm.at[pl.ds(row, 1)], sem)

    for r in range(tm):
        for j in range(TOP_K):
            copy(r, dest_ref[0, 0, TOP_K * r + j]).start(priority=j % 2)
    for r in range(tm):
        for j in range(TOP_K):
            copy(r, 0).wait()


def _dispatch(fill, dest3, n2p):
    n_tiles = dest3.shape[0]
    n_rows = fill.shape[0] * ROW_BLOCK
    grid_spec = pltpu.PrefetchScalarGridSpec(
        num_scalar_prefetch=1,
        grid=(n_tiles,),
        in_specs=[
            pl.BlockSpec((1, 1, TOP_K * TOKEN_TILE), lambda i, fl: (i, 0, 0),
                         memory_space=pltpu.SMEM),
            pl.BlockSpec((TOKEN_TILE, PACKED_WIDTH), lambda i, fl: (i, 0)),
        ],
        out_specs=pl.BlockSpec(memory_space=pl.ANY),
        scratch_shapes=[
            pltpu.VMEM((ROW_BLOCK, PACKED_WIDTH), jnp.uint32),
            pltpu.SemaphoreType.DMA(()),
            pltpu.SemaphoreType.DMA(()),
        ],
    )
    return pl.pallas_call(
        _dispatch_kernel,
        grid_spec=grid_spec,
        out_shape=jax.ShapeDtypeStruct((n_rows, PACKED_WIDTH), jnp.uint32),
        compiler_params=pltpu.CompilerParams(
            dimension_semantics=("arbitrary",), has_side_effects=True),
        name="dispatch",
    )(fill, dest3, n2p)


def _ffn_kernel(sbe_ref, nact_ref, src_ref, same_ref, x_ref, wg_ref, wu_ref, wd_ref, bg_ref,
                bu_ref, bd_ref, o_ref, xb_ref, act_ref, wgb_ref, wub_ref, wdb_ref):
    del sbe_ref, src_ref
    i = pl.program_id(0)
    f = pl.program_id(1)
    nf = D_FF // FFN_TILE
    nact = nact_ref[i]

    full = nact == SUB_PER_SUPER

    def gate_up(sb):
        rows = slice(sb * ROW_BLOCK, (sb + 1) * ROW_BLOCK)
        xs = xb_ref[rows, :]
        gate = jnp.minimum(_dot(xs, wgb_ref[...]) + bg_ref[0], SWIGLU_LIMIT)
        up = jnp.clip(_dot(xs, wub_ref[...]) + bu_ref[0], -SWIGLU_LIMIT, SWIGLU_LIMIT)
        act_ref[f, rows, :] = ((up + 1.0) * gate * _sigmoid(SWIGLU_ALPHA * gate)).astype(BF16)

    def down(sb):
        rows = slice(sb * ROW_BLOCK, (sb + 1) * ROW_BLOCK)
        act = jnp.concatenate([act_ref[k, rows, :] for k in range(nf)], axis=1)
        out = _dot(act, wdb_ref[...]) + bd_ref[0]
        o_ref[rows, :] = _pack_bf16_pairs(out)

    @pl.when(jnp.logical_and(nact > 0, f == 0))
    def _():
        for sb in range(SUB_PER_SUPER):
            rows = slice(sb * ROW_BLOCK, (sb + 1) * ROW_BLOCK)
            lo, hi = _unpack_bf16_pairs(x_ref[rows, :])
            xb_ref[rows, 0:PACKED_WIDTH] = lo.astype(BF16)
            xb_ref[rows, PACKED_WIDTH:D_MODEL] = hi.astype(BF16)

    def cast_weights():
        wgb_ref[...] = wg_ref[0].astype(BF16)
        wub_ref[...] = wu_ref[0].astype(BF16)

        @pl.when(same_ref[i] == 0)
        def _():
            wdb_ref[pl.ds(pl.multiple_of(f * FFN_TILE, FFN_TILE), FFN_TILE), :] = (
                wd_ref[0].astype(BF16))

    @pl.when(full)
    def _():
        cast_weights()
        for sb in range(SUB_PER_SUPER):
            gate_up(sb)

    @pl.when(jnp.logical_and(nact > 0, jnp.logical_not(full)))
    def _():
        cast_weights()
        gate_up(0)

    for sb in range(1, SUB_PER_SUPER - 1):
        @pl.when(jnp.logical_and(sb < nact, jnp.logical_not(full)))
        def _():
            gate_up(sb)

    @pl.when(f == nf - 1)
    def _():
        @pl.when(full)
        def _():
            for sb in range(SUB_PER_SUPER):
                down(sb)

        for sb in range(SUB_PER_SUPER):
            rows = slice(sb * ROW_BLOCK, (sb + 1) * ROW_BLOCK)

            @pl.when(jnp.logical_and(sb < nact, jnp.logical_not(full)))
            def _():
                down(sb)

            @pl.when(sb >= nact)
            def _():
                o_ref[rows, :] = jnp.zeros((ROW_BLOCK, PACKED_WIDTH), jnp.uint32)


def _ffn(sb_expert, sb_nact, sb_src, sb_same, xg, wgu, wd, bgu3, bd3):
    n_super = sb_expert.shape[0]
    nf = D_FF // FFN_TILE

    def tile(f, n, i):
        return jnp.where(n[i] > 0, f, nf - 1)

    def down_tile(f, n, m, i):
        return jnp.where(jnp.logical_and(n[i] > 0, m[i] == 0), f, nf - 1)

    grid_spec = pltpu.PrefetchScalarGridSpec(
        num_scalar_prefetch=4,
        grid=(n_super, nf),
        in_specs=[
            pl.BlockSpec((SUPER_ROWS, PACKED_WIDTH), lambda i, f, e, n, s, m: (s[i], 0)),
            pl.BlockSpec((1, D_MODEL, FFN_TILE),
                         lambda i, f, e, n, s, m: (e[i], 0, tile(f, n, i))),
            pl.BlockSpec((1, D_MODEL, FFN_TILE),
                         lambda i, f, e, n, s, m: (e[i], 0, tile(f, n, i) + nf)),
            pl.BlockSpec((1, FFN_TILE, D_MODEL),
                         lambda i, f, e, n, s, m: (e[i], down_tile(f, n, m, i), 0)),
            pl.BlockSpec((1, 1, FFN_TILE), lambda i, f, e, n, s, m: (e[i], 0, tile(f, n, i))),
            pl.BlockSpec((1, 1, FFN_TILE),
                         lambda i, f, e, n, s, m: (e[i], 0, tile(f, n, i) + nf)),
            pl.BlockSpec((1, 1, D_MODEL), lambda i, f, e, n, s, m: (e[i], 0, 0)),
        ],
        out_specs=pl.BlockSpec((SUPER_ROWS, PACKED_WIDTH), lambda i, f, e, n, s, m: (i, 0)),
        scratch_shapes=[
            pltpu.VMEM((SUPER_ROWS, D_MODEL), BF16),
            pltpu.VMEM((D_FF // FFN_TILE, SUPER_ROWS, FFN_TILE), BF16),
            pltpu.VMEM((D_MODEL, FFN_TILE), BF16),
            pltpu.VMEM((D_MODEL, FFN_TILE), BF16),
            pltpu.VMEM((D_FF, D_MODEL), BF16),
        ],
    )
    return pl.pallas_call(
        _ffn_kernel,
        grid_spec=grid_spec,
        out_shape=jax.ShapeDtypeStruct(xg.shape, jnp.uint32),
        compiler_params=pltpu.CompilerParams(
            dimension_semantics=("arbitrary", "arbitrary"),
            vmem_limit_bytes=VMEM_LIMIT_BYTES),
        name="ffn",
    )(sb_expert, sb_nact, sb_src, sb_same, xg, wgu, wgu, wd, bgu3, bgu3, bd3)


def _combine_kernel(dcur_ref, dnxt_ref, wt_ref, h1_ref, fw_ref, eo_hbm, o_ref, buf, sem):
    tm = TOKEN_TILE
    i = pl.program_id(0)
    n = pl.num_programs(0)
    slot = lax.rem(i, 2)

    def copy(row, s, r, j):
        return pltpu.make_async_copy(
            eo_hbm.at[pl.ds(row, 1)], buf.at[s, j, pl.ds(r, 1)], sem.at[s])

    def start_all(dref, s):
        for r in range(tm):
            for j in range(TOP_K):
                copy(dref[0, 0, TOP_K * r + j], s, r, j).start(priority=j % 2)

    def wait_all(s):
        for r in range(tm):
            for j in range(TOP_K):
                copy(0, s, r, j).wait()

    @pl.when(i == 0)
    def _():
        start_all(dcur_ref, 0)

    for s in range(2):
        @pl.when(jnp.logical_and(i + 1 < n, slot != s))
        def _():
            start_all(dnxt_ref, s)

    for s in range(2):
        @pl.when(slot == s)
        def _():
            wait_all(s)

    wt = wt_ref[...]
    half = PACKED_WIDTH
    y_lo = h1_ref[:, 0:half]
    y_hi = h1_ref[:, half:D_MODEL]
    for j in range(TOP_K):
        lo, hi = _unpack_bf16_pairs(buf[slot, j])
        y_lo = y_lo + lo * wt[:, j:j + 1]
        y_hi = y_hi + hi * wt[:, j:j + 1]
    ssq = (jnp.sum(y_lo * y_lo, axis=-1, keepdims=True)
           + jnp.sum(y_hi * y_hi, axis=-1, keepdims=True))
    scale = lax.rsqrt(ssq * (1.0 / D_MODEL) + NORM_EPS)
    o_ref[:, 0:half] = y_lo * scale * fw_ref[:, 0:half]
    o_ref[:, half:D_MODEL] = y_hi * scale * fw_ref[:, half:D_MODEL]


def _combine(dest3, wt, h1, final_w, eo):
    n_tiles = dest3.shape[0]
    tm = TOKEN_TILE
    m = h1.shape[0]
    smem = lambda fn: pl.BlockSpec((1, 1, TOP_K * tm), fn, memory_space=pltpu.SMEM)
    return pl.pallas_call(
        _combine_kernel,
        grid=(n_tiles,),
        in_specs=[
            smem(lambda i: (i, 0, 0)),
            smem(lambda i: (jnp.minimum(i + 1, n_tiles - 1), 0, 0)),
            pl.BlockSpec((tm, LANES), lambda i: (i, 0)),
            pl.BlockSpec((tm, D_MODEL), lambda i: (i, 0)),
            pl.BlockSpec((1, D_MODEL), lambda i: (0, 0)),
            pl.BlockSpec(memory_space=pl.ANY),
        ],
        out_specs=pl.BlockSpec((tm, D_MODEL), lambda i: (i, 0)),
        out_shape=jax.ShapeDtypeStruct((m, D_MODEL), F32),
        scratch_shapes=[
            pltpu.VMEM((2, TOP_K, tm, PACKED_WIDTH), jnp.uint32),
            pltpu.SemaphoreType.DMA((2,)),
        ],
        compiler_params=pltpu.CompilerParams(
            dimension_semantics=("arbitrary",),
            vmem_limit_bytes=VMEM_LIMIT_BYTES),
        name="combine",
    )(dest3, dest3, wt, h1, final_w, eo)


def _pad_lanes(v, width=LANES):
    return jnp.pad(v, [(0, 0)] * (v.ndim - 1) + [(0, width - v.shape[-1])])


def _layer(h, norm_mix_w, in_proj_w, conv_w, conv_b, dt_bias, a_log, d_skip, ssd_norm_w,
           gate_up_w, gate_up_b, gla_norm_w, out_proj_w, norm_ffn_w, router_w, router_b,
           w_gate_up, b_gate_up, w_down, b_down, out_norm_w):
    batch, seq, _ = h.shape
    m = batch * seq
    x2 = h.reshape(m, D_MODEL)
    L = MIX_CHUNK

    w = in_proj_w.T
    o_xbc = SSD_WIDTH
    o_dt = o_xbc + SSD_CONV_DIM
    o_q = o_dt + SSD_HEADS
    o_k = o_q + GLA_KEY_WIDTH
    o_v = o_k + GLA_KEY_WIDTH
    o_g = o_v + GLA_VALUE_WIDTH
    o_gl = o_g + GLA_VALUE_WIDTH
    tn = 1024
    seg_starts = (0, o_xbc, o_v, o_g, o_xbc + SSD_WIDTH, o_q, o_k)
    seg_widths = (SSD_WIDTH, SSD_WIDTH, GLA_VALUE_WIDTH, GLA_VALUE_WIDTH, SSD_BC_WIDTH,
                  GLA_KEY_WIDTH, GLA_KEY_WIDTH)
    row_offs = jnp.array([s + t for s, wd_ in zip(seg_starts, seg_widths)
                          for t in range(0, wd_, tn)], jnp.int32)
    n_small = SSD_HEADS + GLA_GATE_RANK
    wt_small = jnp.concatenate([w[o_dt:o_q], w[o_gl:o_gl + GLA_GATE_RANK],
                                jnp.zeros((SMALL_WIDTH - n_small, D_MODEL), F32)], axis=0)
    ws_hi, ws_lo = _split2(wt_small)

    u, small = _in_proj(row_offs, x2, norm_mix_w.reshape(1, D_MODEL), w.astype(BF16),
                        ws_hi, ws_lo, tm=min(1024, m), tn=tn)

    tri = jnp.tril(jnp.ones((L, L), F32)).astype(BF16)
    head_of_col = jnp.arange(SSD_WIDTH) // SSD_HEAD_DIM
    expand = (jnp.arange(LANES)[:, None] == head_of_col[None, :]).astype(BF16)
    e2 = jnp.concatenate([expand, expand], axis=0)
    y_ssd = _ssd(u, small, conv_w, conv_b.reshape(1, -1),
                 _pad_lanes(dt_bias.reshape(1, -1)), _pad_lanes(a_log.reshape(1, -1)),
                 jnp.repeat(d_skip, SSD_HEAD_DIM).reshape(1, -1), ssd_norm_w.reshape(1, -1),
                 e2, tri, batch=batch, seq=seq)

    wup = jnp.zeros((SMALL_WIDTH, GLA_KEY_WIDTH), F32)
    wup = wup.at[SMALL_GATE_OFF:SMALL_GATE_OFF + GLA_GATE_RANK].set(gate_up_w)
    wup_hi, wup_lo = _split2(wup)
    y_gla = _gla(u, small, wup_hi, wup_lo, gate_up_b.reshape(1, -1), gla_norm_w.reshape(1, -1),
                 tri, batch=batch, seq=seq)

    tm = TOKEN_TILE
    wr_hi, wr_lo = _split2(_pad_lanes(router_w))
    stri = jnp.tril(jnp.ones((tm, tm), F32), k=-1).astype(BF16)
    wo = out_proj_w.astype(BF16)
    h1, n2, top_idx, top_w, rank, cnt = _out_proj(
        y_ssd, y_gla, x2, wo[:SSD_WIDTH], wo[SSD_WIDTH:], norm_ffn_w.reshape(1, -1),
        wr_hi, wr_lo, _pad_lanes(router_b.reshape(1, -1)), stri, tm=tm)

    counts = cnt[0, :N_EXPERTS]
    n_sb = (counts + SUPER_ROWS - 1) // SUPER_ROWS
    sb_end = jnp.cumsum(n_sb)
    sb_start = sb_end - n_sb
    dest = (sb_start * SUPER_ROWS)[top_idx[:, :TOP_K]] + rank[:, :TOP_K]
    dest3 = dest.astype(jnp.int32).reshape(m // tm, 1, TOP_K * tm)

    n_super = (m * TOP_K + N_EXPERTS * (SUPER_ROWS - 1)) // SUPER_ROWS
    sb_ids = jnp.arange(n_super)
    total_sb = sb_end[-1]
    sb_live = jnp.minimum(sb_ids, total_sb - 1)
    sb_expert = jnp.minimum(jnp.searchsorted(sb_end, sb_live, side="right"), N_EXPERTS - 1)
    rows_left = counts[sb_expert] - (sb_live - sb_start[sb_expert]) * SUPER_ROWS
    sb_nact = jnp.where(sb_ids < total_sb,
                        (jnp.clip(rows_left, 0, SUPER_ROWS) + ROW_BLOCK - 1) // ROW_BLOCK, 0)

    sb_rows = jnp.where(sb_ids < total_sb, jnp.clip(rows_left, 0, SUPER_ROWS), 0)
    sub_rows = sb_rows[:, None] - jnp.arange(SUB_PER_SUPER)[None, :] * ROW_BLOCK
    fill = (sub_rows < ROW_BLOCK).astype(jnp.int32).reshape(-1)
    xg = _dispatch(fill, dest3, n2)
    sb_expert = sb_expert.astype(jnp.int32)
    prev_expert = jnp.concatenate([jnp.full((1,), -1, jnp.int32), sb_expert[:-1]])
    sb_same = jnp.logical_and(sb_expert == prev_expert, sb_ids < total_sb).astype(jnp.int32)
    eo = _ffn(sb_expert, sb_nact.astype(jnp.int32), sb_live.astype(jnp.int32), sb_same,
              xg, w_gate_up, w_down,
              b_gate_up.reshape(N_EXPERTS, 1, 2 * D_FF), b_down.reshape(N_EXPERTS, 1, D_MODEL))
    out = _combine(dest3, top_w, h1, out_norm_w.reshape(1, -1), eo)
    return out.reshape(batch, seq, D_MODEL)


def kernel(x, norm_mix_w, in_proj_w, conv_w, conv_b, dt_bias, a_log, d_skip, ssd_norm_w,
           gate_up_w, gate_up_b, gla_norm_w, out_proj_w, norm_ffn_w, router_w, router_b,
           w_gate_up, b_gate_up, w_down, b_down, final_norm_w):
    assert norm_mix_w.shape[0] == 1
    return _layer(x, norm_mix_w[0], in_proj_w[0], conv_w[0], conv_b[0], dt_bias[0], a_log[0],
                  d_skip[0], ssd_norm_w[0], gate_up_w[0], gate_up_b[0], gla_norm_w[0],
                  out_proj_w[0], norm_ffn_w[0], router_w[0], router_b[0], w_gate_up[0],
                  b_gate_up[0], w_down[0], b_down[0], final_norm_w)
```
